```python
import math
import jax, jax.numpy as jnp
from jax import lax
import numpy as np

D_MODEL = 2048
BATCH = 2
SEQ = 16384
DEPTH = 2
DEC_BATCH = 4
DEC_SEQ = 2048
PAST_LEN = 128

MIX_W = D_MODEL
GROUP_W = MIX_W // 4
HEAD_DIM = 64
N_HEADS = GROUP_W // HEAD_DIM
N_KV = 2
N_REP = N_HEADS // N_KV
KV_W = N_KV * HEAD_DIM
SCONV_K = 3
CONF_K = 31
WINDOW = 128
WB = 128
QB = 128
GRID_W = 64
ROPE_THETA = 10000.0
D_FF = 4 * D_MODEL
RMS_EPS = 1e-6
LN_EPS = 1e-5
SPLIT_SIZES = (GROUP_W, GROUP_W, GROUP_W,
               GROUP_W, KV_W, KV_W,
               GROUP_W, KV_W, KV_W,
               GROUP_W, GROUP_W)
IN_W = sum(SPLIT_SIZES)

kernel_name = 'hymba_style_hybrid_encoder'


def _rms(x, g):
    xf = x.astype(jnp.float32)
    y = xf * lax.rsqrt(jnp.mean(xf * xf, axis=-1, keepdims=True) + RMS_EPS)
    return (y * g.astype(jnp.float32)).astype(x.dtype)


def _layernorm(x, g, b):
    xf = x.astype(jnp.float32)
    mu = jnp.mean(xf, axis=-1, keepdims=True)
    var = jnp.mean(jnp.square(xf - mu), axis=-1, keepdims=True)
    y = (xf - mu) * lax.rsqrt(var + LN_EPS)
    return (y * g.astype(jnp.float32) + b.astype(jnp.float32)).astype(x.dtype)


def _rope_cos_sin(pos, dim):
    inv = 1.0 / (ROPE_THETA ** (jnp.arange(0, dim, 2, dtype=jnp.float32) / dim))
    ang = pos.astype(jnp.float32)[:, None] * inv[None, :]
    return jnp.cos(ang), jnp.sin(ang)


def _apply_rope(x, cos, sin):
    xf = x.astype(jnp.float32)
    half = xf.shape[-1] // 2
    x1, x2 = xf[..., :half], xf[..., half:]
    c = cos[None, :, None, :]
    s = sin[None, :, None, :]
    return jnp.concatenate([x1 * c - x2 * s, x2 * c + x1 * s], axis=-1).astype(x.dtype)


def _dwconv(x, w, b=None):
    k, c = w.shape
    y = lax.conv_general_dilated(x, w.astype(x.dtype)[:, None, :], window_strides=(1,),
                                 padding=[(k // 2, k // 2)],
                                 dimension_numbers=('NWC', 'WIO', 'NWC'),
                                 feature_group_count=c)
    if b is not None:
        y = y + b.astype(x.dtype)
    return y


def _global_attn(q, k, v):
    b, s, _, d = q.shape
    nb = s // QB
    scale = 1.0 / math.sqrt(d)
    qb = (q * scale).reshape(b, nb, QB, N_KV, N_REP, d).transpose(1, 0, 2, 3, 4, 5)

    def block(qi):
        sc = jnp.einsum('bqkgd,bskd->bkgqs', qi, k).astype(jnp.float32)
        p = jax.nn.softmax(sc, axis=-1).astype(v.dtype)
        return jnp.einsum('bkgqs,bskd->bqkgd', p, v)

    o = lax.map(block, qb)
    return o.transpose(1, 0, 2, 3, 4, 5).reshape(b, s, N_HEADS * d)


def _window_attn(q, k, v, sink):
    b, s, _, d = q.shape
    nb = s // WB
    scale = 1.0 / math.sqrt(d)
    qb = (q * scale).reshape(b, nb, WB, N_KV, N_REP, d)

    def band(t):
        tp = jnp.pad(t, ((0, 0), (WB, WB), (0, 0), (0, 0))).reshape(b, nb + 2, WB, N_KV, d)
        return jnp.concatenate([tp[:, :-2], tp[:, 1:-1], tp[:, 2:]], axis=2)

    kw, vw = band(k), band(v)
    n = jnp.arange(nb)[:, None, None]
    i = jnp.arange(WB)[None, :, None]
    j = jnp.arange(3 * WB)[None, None, :]
    qpos = n * WB + i
    kpos = n * WB - WB + j
    valid = (jnp.abs(qpos - kpos) <= WINDOW) & (kpos >= 0) & (kpos < s)
    sc = jnp.einsum('bnqkgd,bnskd->bnkgqs', qb, kw).astype(jnp.float32)
    sc = jnp.where(valid[None, :, None, None, :, :], sc, -1e30)
    sk = sink.astype(jnp.float32).reshape(1, 1, N_KV, N_REP, 1, 1)
    m = jnp.maximum(jnp.max(sc, axis=-1, keepdims=True), sk)
    e = jnp.exp(sc - m)
    p = (e / (jnp.sum(e, axis=-1, keepdims=True) + jnp.exp(sk - m))).astype(v.dtype)
    o = jnp.einsum('bnkgqs,bnskd->bnqkgd', p, vw)
    return o.reshape(b, s, N_HEADS * d)


def _layer(x, pre_mix_g, w_in, conv_a_w, q_norm_g, k_norm_g, sink_c, conv_d_w, conv_d_b,
           ln_d_g, ln_d_b, w_out, post_mix_g, pre_ffn_g, w_ff1, w_ff2, post_ffn_g):
    b, s, _ = x.shape
    rows = s // GRID_W
    row_pos = jnp.repeat(jnp.arange(rows), GRID_W)
    col_pos = jnp.tile(jnp.arange(GRID_W), rows)
    lin_pos = jnp.arange(s)

    h = _rms(x, pre_mix_g)
    z = jnp.einsum('bsd,de->bse', h, w_in.astype(h.dtype))
    idx = np.cumsum(SPLIT_SIZES)[:-1].tolist()
    (a_b, a_c, a_v, qB, kB, vB, qC, kC, vC, d_a, d_g) = jnp.split(z, idx, axis=-1)

    y_a = a_b * _dwconv(a_c * a_v, conv_a_w)

    qB = _rms(qB.reshape(b, s, N_HEADS, HEAD_DIM), q_norm_g)
    kB = _rms(kB.reshape(b, s, N_KV, HEAD_DIM), k_norm_g)
    vB = vB.reshape(b, s, N_KV, HEAD_DIM)
    half = HEAD_DIM // 2
    rc, rs = _rope_cos_sin(row_pos, half)
    cc, cs = _rope_cos_sin(col_pos, half)

    def axial(t):
        return jnp.concatenate([_apply_rope(t[..., :half], rc, rs),
                                _apply_rope(t[..., half:], cc, cs)], axis=-1)

    y_b = _global_attn(axial(qB), axial(kB), vB)

    lc, ls = _rope_cos_sin(lin_pos, HEAD_DIM)
    qC = _apply_rope(qC.reshape(b, s, N_HEADS, HEAD_DIM), lc, ls)
    kC = _apply_rope(kC.reshape(b, s, N_KV, HEAD_DIM), lc, ls)
    vC = vC.reshape(b, s, N_KV, HEAD_DIM)
    y_c = _window_attn(qC, kC, vC, sink_c)

    u = d_a * jax.nn.sigmoid(d_g)
    u = _dwconv(u, conv_d_w, conv_d_b)
    y_d = jax.nn.silu(_layernorm(u, ln_d_g, ln_d_b))

    y = jnp.concatenate([y_a, y_b, y_c, y_d], axis=-1)
    y = jnp.einsum('bse,ed->bsd', y, w_out.astype(y.dtype))
    x = x + _rms(y, post_mix_g)

    h = _rms(x, pre_ffn_g)
    u = jnp.square(jax.nn.relu(jnp.einsum('bsd,df->bsf', h, w_ff1.astype(h.dtype))))
    y = jnp.einsum('bsf,fd->bsd', u, w_ff2.astype(u.dtype))
    return x + _rms(y, post_ffn_g)


def setup_inputs(seed: int = 0) -> dict:
    key = jax.random.key(seed)
    ks = jax.random.split(key, 24)
    f32 = jnp.float32

    def nrm(k, shape, scale):
        return jax.random.normal(k, shape, f32) * scale

    def gain(k, shape):
        return 1.0 + 0.05 * jax.random.normal(k, shape, f32)

    return {
        'x_prompt': nrm(ks[0], (BATCH, SEQ, D_MODEL), 1.0),
        'x_sample': nrm(ks[1], (DEC_BATCH, DEC_SEQ, D_MODEL), 1.0),
        'pre_mix_g': gain(ks[2], (DEPTH, D_MODEL)),
        'w_in': nrm(ks[3], (DEPTH, D_MODEL, IN_W), D_MODEL ** -0.5),
        'conv_a_w': nrm(ks[4], (DEPTH, SCONV_K, GROUP_W), SCONV_K ** -0.5),
        'q_norm_g': gain(ks[5], (DEPTH, HEAD_DIM)),
        'k_norm_g': gain(ks[6], (DEPTH, HEAD_DIM)),
        'sink_c': nrm(ks[7], (DEPTH, N_HEADS), 0.5),
        'conv_d_w': nrm(ks[8], (DEPTH, CONF_K, GROUP_W), CONF_K ** -0.5),
        'conv_d_b': nrm(ks[9], (DEPTH, GROUP_W), 0.02),
        'ln_d_g': gain(ks[10], (DEPTH, GROUP_W)),
        'ln_d_b': nrm(ks[11], (DEPTH, GROUP_W), 0.02),
        'w_out': nrm(ks[12], (DEPTH, MIX_W, D_MODEL), MIX_W ** -0.5),
        'post_mix_g': gain(ks[13], (DEPTH, D_MODEL)),
        'pre_ffn_g': gain(ks[14], (DEPTH, D_MODEL)),
        'w_ff1': nrm(ks[15], (DEPTH, D_MODEL, D_FF), D_MODEL ** -0.5),
        'w_ff2': nrm(ks[16], (DEPTH, D_FF, D_MODEL), D_FF ** -0.5),
        'post_ffn_g': gain(ks[17], (DEPTH, D_MODEL)),
    }


def reference(x_prompt, x_sample, pre_mix_g, w_in, conv_a_w, q_norm_g, k_norm_g, sink_c,
              conv_d_w, conv_d_b, ln_d_g, ln_d_b, w_out, post_mix_g, pre_ffn_g, w_ff1,
              w_ff2, post_ffn_g):
    yp = x_prompt
    ys = x_sample
    for l in range(DEPTH):
        params = (pre_mix_g[l], w_in[l], conv_a_w[l], q_norm_g[l], k_norm_g[l], sink_c[l],
                  conv_d_w[l], conv_d_b[l], ln_d_g[l], ln_d_b[l], w_out[l], post_mix_g[l],
                  pre_ffn_g[l], w_ff1[l], w_ff2[l], post_ffn_g[l])
        yp = _layer(yp, *params)
        ys = _layer(ys, *params)
    return (yp, ys)
```

```python
import functools
import math

import jax
import jax.numpy as jnp
import numpy as np
from jax import lax
from jax.experimental import pallas as pl
from jax.experimental.pallas import tpu as pltpu

F32 = jnp.float32
BF16 = jnp.bfloat16

D_MODEL = 2048
GROUP_W = 512
HEAD_DIM = 64
N_HEADS = 8
N_KV = 2
N_REP = 4
KV_W = N_KV * HEAD_DIM
SCONV_K = 3
CONF_K = 31
WINDOW = 128
GRID_W = 64
ROPE_THETA = 10000.0
D_FF = 4 * D_MODEL
IN_W = 4096
RMS_EPS = 1e-6
LN_EPS = 1e-5
NEG_BIG = -1e30

LANES = 128
BF16_SUBLANES = 16
VMEM_LIMIT_BYTES = 56 * 1024 * 1024

COL_AB, COL_AC, COL_AV, COL_DA, COL_DG, COL_QB, COL_QC = (0, 512, 1024, 1536, 2048, 2560, 3072)
COL_KB, COL_VB, COL_KC, COL_VC = (3584, 3712, 3840, 3968)
_ORIG = dict(ab=0, ac=512, av=1024, qb=1536, kb=2048, vb=2176, qc=2304, kc=2816, vc=2944, da=3072, dg=3584)

IN_TN = 1024
ATT_TQ = 256
ATT_TK = 256
ATT_M = N_REP * ATT_TQ
WIN_TK = ATT_TQ + 2 * WINDOW
V_ROWS = HEAD_DIM + BF16_SUBLANES
MIX_TM = 512
MIX_CH = 64
HALO = 16
OUT_TM = 512
FFN_TM = 512
FFN_TF = 512


def _cparams(sem):
    return pltpu.CompilerParams(dimension_semantics=sem, vmem_limit_bytes=VMEM_LIMIT_BYTES)


def _rope(x, cos, sin_signed, half, use_fwd):
    partner = jnp.where(use_fwd, pltpu.roll(x, half, 1), pltpu.roll(x, LANES - half, 1))
    return x * cos + partner * sin_signed


def _head_rms(x, gain, ones_bd):
    sq = x * x
    hi = sq.astype(BF16)
    lo = (sq - hi.astype(F32)).astype(BF16)
    ssum = (jnp.dot(hi, ones_bd, preferred_element_type=F32)
            + jnp.dot(lo, ones_bd, preferred_element_type=F32))
    return (x * lax.rsqrt(ssum * (1.0 / HEAD_DIM) + RMS_EPS)) * gain


def _in_proj_kernel(x_ref, g_ref, w_ref, cax_ref, sax_ref, cli_ref, sli_ref, qg_ref, kg_ref,
                    ones_ref, z_ref, h_ref):
    j = pl.program_id(1)

    @pl.when(j == 0)
    def _():
        x = x_ref[...]
        ms = jnp.mean(x * x, axis=-1, keepdims=True)
        h_ref[...] = ((x * lax.rsqrt(ms + RMS_EPS)) * g_ref[...]).astype(BF16)

    acc = jnp.dot(h_ref[...], w_ref[...], preferred_element_type=F32)

    @pl.when(j < 2)
    def _():
        z_ref[...] = acc.astype(BF16)

    lane = lax.broadcasted_iota(jnp.int32, (1, LANES), 1)
    fwd_ax = (lane % 32) >= 16
    fwd_li = (lane % 64) >= 32

    @pl.when(j == 2)
    def _():
        z_ref[:, 0:512] = acc[:, 0:512].astype(BF16)
        cax, sax, ones_bd, qg = cax_ref[...], sax_ref[...], ones_ref[...], qg_ref[...]
        for c in range(4):
            lo = 512 + c * LANES
            x = _head_rms(acc[:, lo:lo + LANES], qg, ones_bd)
            x = _rope(x, cax, sax, 16, fwd_ax) * (1.0 / math.sqrt(HEAD_DIM))
            z_ref[:, lo:lo + LANES] = x.astype(BF16)

    @pl.when(j == 3)
    def _():
        cax, sax, ones_bd, kg = cax_ref[...], sax_ref[...], ones_ref[...], kg_ref[...]
        cli, sli = cli_ref[...], sli_ref[...]
        for c in range(4):
            lo = c * LANES
            x = _rope(acc[:, lo:lo + LANES], cli, sli, 32, fwd_li) * (1.0 / math.sqrt(HEAD_DIM))
            z_ref[:, lo:lo + LANES] = x.astype(BF16)
        kb = _rope(_head_rms(acc[:, 512:640], kg, ones_bd), cax, sax, 16, fwd_ax)
        z_ref[:, 512:640] = kb.astype(BF16)
        z_ref[:, 640:768] = acc[:, 640:768].astype(BF16)
        z_ref[:, 768:896] = _rope(acc[:, 768:896], cli, sli, 32, fwd_li).astype(BF16)
        z_ref[:, 896:1024] = acc[:, 896:1024].astype(BF16)


def _in_proj(x2d, seq, g, w_bf, tabs, qg, kg, ones_bd):
    t = x2d.shape[0]
    tm = min(1024, seq)
    nseq = seq // tm
    row_spec = lambda width: pl.BlockSpec((tm, width), lambda i, j: (i % nseq, 0))
    const = lambda shape: pl.BlockSpec(shape, lambda i, j: (0, 0))
    return pl.pallas_call(
        _in_proj_kernel,
        grid=(t // tm, IN_W // IN_TN),
        in_specs=[
            pl.BlockSpec((tm, D_MODEL), lambda i, j: (i, 0)),
            const((1, D_MODEL)),
            pl.BlockSpec((D_MODEL, IN_TN), lambda i, j: (0, j)),
            row_spec(LANES), row_spec(LANES), row_spec(LANES), row_spec(LANES),
            const((1, LANES)), const((1, LANES)), const((LANES, LANES)),
        ],
        out_specs=pl.BlockSpec((tm, IN_TN), lambda i, j: (i, j)),
        out_shape=jax.ShapeDtypeStruct((t, IN_W), BF16),
        scratch_shapes=[pltpu.VMEM((tm, D_MODEL), BF16)],
        compiler_params=_cparams(("arbitrary", "arbitrary")),
        name="in_proj",
    )(x2d, g, w_bf, tabs[0], tabs[1], tabs[2], tabs[3], qg, kg, ones_bd)


def _build_qpad(q_ref, qpad_ref):
    qt = q_ref[...].astype(F32).T
    zeros = jnp.zeros((HEAD_DIM, ATT_M), BF16)
    for kvh in range(N_KV):
        qpad_ref[kvh, (1 - kvh) * HEAD_DIM:(2 - kvh) * HEAD_DIM, :] = zeros
        for g in range(N_REP):
            h = N_REP * kvh + g
            qpad_ref[kvh, kvh * HEAD_DIM:(kvh + 1) * HEAD_DIM, g * ATT_TQ:(g + 1) * ATT_TQ] = (
                qt[h * HEAD_DIM:(h + 1) * HEAD_DIM, :].astype(BF16))


def _store_heads(o_ref, ot_ref, o_by_kvh):
    for kvh in range(N_KV):
        for g in range(N_REP):
            h = N_REP * kvh + g
            ot_ref[h * HEAD_DIM:(h + 1) * HEAD_DIM, :] = o_by_kvh[kvh][:, g * ATT_TQ:(g + 1) * ATT_TQ]
    o_ref[...] = ot_ref[...].T.astype(BF16)


def _global_attn_kernel(q_ref, k_ref, v_ref, o_ref, qpad_ref, vt_ref, m_ref, acc_ref, ot_ref, *, nkb):
    i = pl.program_id(1)

    @pl.when(i == 0)
    def _():
        ones = jnp.ones((BF16_SUBLANES, ATT_TK), BF16)

        def fill(c, carry):
            r0 = pl.multiple_of(c * ATT_TK, ATT_TK)
            vt = v_ref[pl.ds(r0, ATT_TK), :].astype(F32).T
            for kvh in range(N_KV):
                vt_ref[kvh, c, 0:HEAD_DIM, :] = vt[kvh * HEAD_DIM:(kvh + 1) * HEAD_DIM, :].astype(BF16)
                vt_ref[kvh, c, HEAD_DIM:V_ROWS, :] = ones
            return carry

        lax.fori_loop(0, nkb, fill, 0)

    _build_qpad(q_ref, qpad_ref)
    m_ref[...] = jnp.full(m_ref.shape, NEG_BIG, F32)
    acc_ref[...] = jnp.zeros(acc_ref.shape, F32)

    def body(c, carry):
        r0 = pl.multiple_of(c * ATT_TK, ATT_TK)
        kb = k_ref[pl.ds(r0, ATT_TK), :]
        for kvh in range(N_KV):
            s = jnp.dot(kb, qpad_ref[kvh], preferred_element_type=F32)
            m_old = m_ref[kvh]
            m_new = jnp.maximum(m_old, jnp.max(s, axis=0, keepdims=True))
            alpha = jnp.exp(m_old - m_new)
            p = jnp.exp(s - m_new).astype(BF16)
            pv = jnp.dot(vt_ref[kvh, c], p, preferred_element_type=F32)
            acc_ref[kvh] = acc_ref[kvh] * alpha + pv
            m_ref[kvh] = m_new
        return carry

    lax.fori_loop(0, nkb, body, 0)

    outs = []
    for kvh in range(N_KV):
        acc = acc_ref[kvh]
        outs.append(acc[0:HEAD_DIM, :] / acc[HEAD_DIM:HEAD_DIM + 1, :])
    _store_heads(o_ref, ot_ref, outs)


def _global_attn(z, batch, seq):
    nq = seq // ATT_TQ
    nkb = seq // ATT_TK
    return pl.pallas_call(
        functools.partial(_global_attn_kernel, nkb=nkb),
        grid=(batch, nq),
        in_specs=[
            pl.BlockSpec((ATT_TQ, GROUP_W), lambda b, i: (b * nq + i, COL_QB // GROUP_W)),
            pl.BlockSpec((seq, KV_W), lambda b, i: (b, COL_KB // KV_W)),
            pl.BlockSpec((seq, KV_W), lambda b, i: (b, COL_VB // KV_W)),
        ],
        out_specs=pl.BlockSpec((ATT_TQ, GROUP_W), lambda b, i: (b * nq + i, 0)),
        out_shape=jax.ShapeDtypeStruct((batch * seq, GROUP_W), BF16),
        scratch_shapes=[
            pltpu.VMEM((N_KV, KV_W, ATT_M), BF16),
            pltpu.VMEM((N_KV, nkb, V_ROWS, ATT_TK), BF16),
            pltpu.VMEM((N_KV, 1, ATT_M), F32),
            pltpu.VMEM((N_KV, V_ROWS, ATT_M), F32),
            pltpu.VMEM((GROUP_W, ATT_TQ), F32),
        ],
        compiler_params=_cparams(("arbitrary", "arbitrary")),
        name="global_attn",
    )(z, z, z)


def _window_attn_kernel(q_ref, k_ref, v_ref, sink_ref, o_ref, qpad_ref, ot_ref, *, seq):
    i = pl.program_id(1)
    _build_qpad(q_ref, qpad_ref)
    q0 = i * ATT_TQ
    start = pl.multiple_of(jnp.clip(q0 - WINDOW, 0, seq - WIN_TK), WINDOW)
    kb = k_ref[pl.ds(start, WIN_TK), :]
    vt = v_ref[pl.ds(start, WIN_TK), :].astype(F32).T
    kpos = start + lax.broadcasted_iota(jnp.int32, (WIN_TK, ATT_M), 0)
    qpos = q0 + lax.broadcasted_iota(jnp.int32, (WIN_TK, ATT_M), 1) % ATT_TQ
    valid = jnp.abs(qpos - kpos) <= WINDOW
    outs = []
    for kvh in range(N_KV):
        s = jnp.dot(kb, qpad_ref[kvh], preferred_element_type=F32)
        s = jnp.where(valid, s, NEG_BIG)
        sk = sink_ref[kvh]
        m = jnp.maximum(jnp.max(s, axis=0, keepdims=True), sk)
        e = jnp.exp(s - m)
        denom = jnp.sum(e, axis=0, keepdims=True) + jnp.exp(sk - m)
        p = (e / denom).astype(BF16)
        vk = vt[kvh * HEAD_DIM:(kvh + 1) * HEAD_DIM, :].astype(BF16)
        outs.append(jnp.dot(vk, p, preferred_element_type=F32))
    _store_heads(o_ref, ot_ref, outs)


def _window_attn(z, sink_rows, batch, seq):
    nq = seq // ATT_TQ
    return pl.pallas_call(
        functools.partial(_window_attn_kernel, seq=seq),
        grid=(batch, nq),
        in_specs=[
            pl.BlockSpec((ATT_TQ, GROUP_W), lambda b, i: (b * nq + i, COL_QC // GROUP_W)),
            pl.BlockSpec((seq, KV_W), lambda b, i: (b, COL_KC // KV_W)),
            pl.BlockSpec((seq, KV_W), lambda b, i: (b, COL_VC // KV_W)),
            pl.BlockSpec((N_KV, 1, ATT_M), lambda b, i: (0, 0, 0)),
        ],
        out_specs=pl.BlockSpec((ATT_TQ, GROUP_W), lambda b, i: (b * nq + i, 0)),
        out_shape=jax.ShapeDtypeStruct((batch * seq, GROUP_W), BF16),
        scratch_shapes=[
            pltpu.VMEM((N_KV, KV_W, ATT_M), BF16),
            pltpu.VMEM((GROUP_W, ATT_TQ), F32),
        ],
        compiler_params=_cparams(("arbitrary", "arbitrary")),
        name="window_attn",
    )(z, z, z, sink_rows)


def _mixer_kernel(ab_ref, ac_ref, acp_ref, acn_ref, av_ref, avp_ref, avn_ref,
                  da_ref, dap_ref, dan_ref, dg_ref, dgp_ref, dgn_ref,
                  wa_ref, wd_ref, bd_ref, lng_ref, lnb_ref,
                  ya_ref, yd_ref, pbuf, ubuf, *, nseq):
    i = pl.program_id(0)
    keep_prev = jnp.where(i % nseq == 0, 0.0, 1.0)
    keep_next = jnp.where(i % nseq == nseq - 1, 0.0, 1.0)

    def gate(a_ref, g_ref):
        return a_ref[...].astype(F32) * jax.nn.sigmoid(g_ref[...].astype(F32))

    def prod(a_ref, b_ref):
        return a_ref[...].astype(F32) * b_ref[...].astype(F32)

    tm = MIX_TM
    ubuf[0:HALO, :] = gate(dap_ref, dgp_ref) * keep_prev
    ubuf[HALO:HALO + tm, :] = gate(da_ref, dg_ref)
    ubuf[HALO + tm:2 * HALO + tm, :] = gate(dan_ref, dgn_ref) * keep_next
    pbuf[0:HALO, :] = prod(acp_ref, avp_ref) * keep_prev
    pbuf[HALO:HALO + tm, :] = prod(ac_ref, av_ref)
    pbuf[HALO + tm:2 * HALO + tm, :] = prod(acn_ref, avn_ref) * keep_next

    bd, lng, lnb = bd_ref[...], lng_ref[...], lnb_ref[...]

    for c in range(tm // MIX_CH):
        r0 = c * MIX_CH
        acc = jnp.zeros((MIX_CH, GROUP_W), F32)
        for k in range(CONF_K):
            lo = r0 + HALO - CONF_K // 2 + k
            acc = acc + wd_ref[k:k + 1, :] * ubuf[lo:lo + MIX_CH, :]
        acc = acc + bd
        mu = jnp.mean(acc, axis=-1, keepdims=True)
        xc = acc - mu
        var = jnp.mean(xc * xc, axis=-1, keepdims=True)
        y = (xc * lax.rsqrt(var + LN_EPS)) * lng + lnb
        yd_ref[r0:r0 + MIX_CH, :] = (y * jax.nn.sigmoid(y)).astype(BF16)
        pa = jnp.zeros((MIX_CH, GROUP_W), F32)
        for k in range(SCONV_K):
            lo = r0 + HALO - SCONV_K // 2 + k
            pa = pa + wa_ref[k:k + 1, :] * pbuf[lo:lo + MIX_CH, :]
        ya_ref[r0:r0 + MIX_CH, :] = (ab_ref[r0:r0 + MIX_CH, :].astype(F32) * pa).astype(BF16)


def _mixer(z, seq, wa, wd, bd, lng, lnb):
    t = z.shape[0]
    tm = MIX_TM
    nseq = seq // tm
    hb = tm // HALO
    nh = t // HALO

    def cur(col):
        return pl.BlockSpec((tm, GROUP_W), lambda i: (i, col // GROUP_W))

    def prev(col):
        return pl.BlockSpec((HALO, GROUP_W), lambda i: (jnp.maximum(i * hb - 1, 0), col // GROUP_W))

    def nxt(col):
        return pl.BlockSpec((HALO, GROUP_W), lambda i: (jnp.minimum((i + 1) * hb, nh - 1), col // GROUP_W))

    const = lambda shape: pl.BlockSpec(shape, lambda i: (0, 0))
    halo_cols = (COL_AC, COL_AV, COL_DA, COL_DG)
    in_specs = [cur(COL_AB)]
    for col in halo_cols:
        in_specs += [cur(col), prev(col), nxt(col)]
    in_specs += [const((SCONV_K, GROUP_W)), const((CONF_K, GROUP_W)), const((1, GROUP_W)),
                 const((1, GROUP_W)), const((1, GROUP_W))]
    out_spec = pl.BlockSpec((tm, GROUP_W), lambda i: (i, 0))
    return pl.pallas_call(
        functools.partial(_mixer_kernel, nseq=nseq),
        grid=(t // tm,),
        in_specs=in_specs,
        out_specs=[out_spec, out_spec],
        out_shape=[jax.ShapeDtypeStruct((t, GROUP_W), BF16)] * 2,
        scratch_shapes=[pltpu.VMEM((tm + 2 * HALO, GROUP_W), F32),
                        pltpu.VMEM((tm + 2 * HALO, GROUP_W), F32)],
        compiler_params=_cparams(("arbitrary",)),
        name="mixer",
    )(*([z] * 13), wa, wd, bd, lng, lnb)


def _out_proj_kernel(ya_ref, yb_ref, yc_ref, yd_ref, w_ref, x_ref, g_ref, o_ref):
    y = jnp.concatenate([ya_ref[...], yb_ref[...], yc_ref[...], yd_ref[...]], axis=-1)
    acc = jnp.dot(y, w_ref[...], preferred_element_type=F32)
    ms = jnp.mean(acc * acc, axis=-1, keepdims=True)
    o_ref[...] = x_ref[...] + (acc * lax.rsqrt(ms + RMS_EPS)) * g_ref[...]


def _out_proj(ya, yb, yc, yd, w_bf, x2d, g):
    t = x2d.shape[0]
    tm = OUT_TM
    yspec = pl.BlockSpec((tm, GROUP_W), lambda i: (i, 0))
    xspec = pl.BlockSpec((tm, D_MODEL), lambda i: (i, 0))
    return pl.pallas_call(
        _out_proj_kernel,
        grid=(t // tm,),
        in_specs=[yspec, yspec, yspec, yspec,
                  pl.BlockSpec((D_MODEL, D_MODEL), lambda i: (0, 0)),
                  xspec,
                  pl.BlockSpec((1, D_MODEL), lambda i: (0, 0))],
        out_specs=xspec,
        out_shape=jax.ShapeDtypeStruct((t, D_MODEL), F32),
        compiler_params=_cparams(("arbitrary",)),
        name="out_proj",
    )(ya, yb, yc, yd, w_bf, x2d, g)


def _ffn_kernel(x_ref, g1_ref, w1_ref, w2_ref, g2_ref, o_ref, h_ref, acc_ref, *, nf):
    j = pl.program_id(1)

    @pl.when(j == 0)
    def _():
        x = x_ref[...]
        ms = jnp.mean(x * x, axis=-1, keepdims=True)
        h_ref[...] = ((x * lax.rsqrt(ms + RMS_EPS)) * g1_ref[...]).astype(BF16)
        acc_ref[...] = jnp.zeros(acc_ref.shape, F32)

    u = jnp.dot(h_ref[...], w1_ref[...], preferred_element_type=F32)
    u = jnp.maximum(u, 0.0)
    u = (u * u).astype(BF16)
    acc_ref[...] += jnp.dot(u, w2_ref[...], preferred_element_type=F32)

    @pl.when(j == nf - 1)
    def _():
        y = acc_ref[...]
        ms = jnp.mean(y * y, axis=-1, keepdims=True)
        o_ref[...] = x_ref[...] + (y * lax.rsqrt(ms + RMS_EPS)) * g2_ref[...]


def _ffn(x2d, g1, w1_bf, w2_bf, g2):
    t = x2d.shape[0]
    tm, tf = FFN_TM, FFN_TF
    nf = D_FF // tf
    xspec = pl.BlockSpec((tm, D_MODEL), lambda i, j: (i, 0))
    gspec = pl.BlockSpec((1, D_MODEL), lambda i, j: (0, 0))
    return pl.pallas_call(
        functools.partial(_ffn_kernel, nf=nf),
        grid=(t // tm, nf),
        in_specs=[xspec, gspec,
                  pl.BlockSpec((D_MODEL, tf), lambda i, j: (0, j)),
                  pl.BlockSpec((tf, D_MODEL), lambda i, j: (j, 0)),
                  gspec],
        out_specs=xspec,
        out_shape=jax.ShapeDtypeStruct((t, D_MODEL), F32),
        scratch_shapes=[pltpu.VMEM((tm, D_MODEL), BF16), pltpu.VMEM((tm, D_MODEL), F32)],
        compiler_params=_cparams(("arbitrary", "arbitrary")),
        name="ffn",
    )(x2d, g1, w1_bf, w2_bf, g2)


def _rope_tables(seq):
    lane = np.arange(LANES)
    t = jnp.arange(seq)
    inv_ax = 1.0 / (ROPE_THETA ** (jnp.arange(0, 32, 2, dtype=F32) / 32))
    pos_row = (t // GRID_W).astype(F32)
    pos_col = (t % GRID_W).astype(F32)
    use_col = ((lane % HEAD_DIM) >= 32)
    pos = jnp.where(use_col[None, :], pos_col[:, None], pos_row[:, None])
    ang = pos * inv_ax[lane % 16][None, :]
    sign_ax = np.where((lane % 32) >= 16, 1.0, -1.0).astype(np.float32)
    cax, sax = jnp.cos(ang), jnp.sin(ang) * sign_ax[None, :]
    inv_li = 1.0 / (ROPE_THETA ** (jnp.arange(0, HEAD_DIM, 2, dtype=F32) / HEAD_DIM))
    ang = t.astype(F32)[:, None] * inv_li[lane % 32][None, :]
    sign_li = np.where((lane % HEAD_DIM) >= 32, 1.0, -1.0).astype(np.float32)
    cli, sli = jnp.cos(ang), jnp.sin(ang) * sign_li[None, :]
    return cax, sax, cli, sli


def _permute_w_in(w):
    o = _ORIG
    order = [("ab", 512), ("ac", 512), ("av", 512), ("da", 512), ("dg", 512), ("qb", 512), ("qc", 512),
             ("kb", 128), ("vb", 128), ("kc", 128), ("vc", 128)]
    return jnp.concatenate([w[:, o[n]:o[n] + wd] for n, wd in order], axis=1).astype(BF16)


def _layer(x2d, batch, seq, tabs, ones_bd, p):
    z = _in_proj(x2d, seq, p["pre_mix_g"], p["w_in"], tabs, p["q_norm_g"], p["k_norm_g"], ones_bd)
    yb = _global_attn(z, batch, seq)
    yc = _window_attn(z, p["sink_rows"], batch, seq)
    ya, yd = _mixer(z, seq, p["conv_a_w"], p["conv_d_w"], p["conv_d_b"], p["ln_d_g"], p["ln_d_b"])
    x1 = _out_proj(ya, yb, yc, yd, p["w_out"], x2d, p["post_mix_g"])
    return _ffn(x1, p["pre_ffn_g"], p["w_ff1"], p["w_ff2"], p["post_ffn_g"])


def kernel(x_prompt, x_sample, pre_mix_g, w_in, conv_a_w, q_norm_g, k_norm_g, sink_c, conv_d_w, conv_d_b,
           ln_d_g, ln_d_b, w_out, post_mix_g, pre_ffn_g, w_ff1, w_ff2, post_ffn_g):
    depth = w_in.shape[0]
    lane = np.arange(LANES)
    ones_bd = jnp.asarray((lane[:, None] // HEAD_DIM) == (lane[None, :] // HEAD_DIM), BF16)
    layers = []
    for l in range(depth):
        row = lambda a: a[l].reshape(1, -1).astype(F32)
        sink_rows = jnp.repeat(sink_c[l].astype(F32).reshape(N_KV, 1, N_REP), ATT_TQ, axis=2)
        layers.append(dict(
            pre_mix_g=row(pre_mix_g), w_in=_permute_w_in(w_in[l]),
            conv_a_w=conv_a_w[l].astype(F32), conv_d_w=conv_d_w[l].astype(F32),
            conv_d_b=row(conv_d_b), ln_d_g=row(ln_d_g), ln_d_b=row(ln_d_b),
            q_norm_g=jnp.tile(row(q_norm_g), (1, 2)), k_norm_g=jnp.tile(row(k_norm_g), (1, 2)),
            sink_rows=sink_rows, w_out=w_out[l].astype(BF16), post_mix_g=row(post_mix_g),
            pre_ffn_g=row(pre_ffn_g), w_ff1=w_ff1[l].astype(BF16), w_ff2=w_ff2[l].astype(BF16),
            post_ffn_g=row(post_ffn_g)))
    outs = []
    for x in (x_prompt, x_sample):
        batch, seq, _ = x.shape
        tabs = _rope_tables(seq)
        y = x.reshape(batch * seq, D_MODEL)
        for p in layers:
            y = _layer(y, batch, seq, tabs, ones_bd, p)
        outs.append(y.reshape(batch, seq, D_MODEL))
    return tuple(outs)
```

```python
import functools
import math

import jax
import jax.numpy as jnp
import numpy as np
from jax import lax
from jax.experimental import pallas as pl
from jax.experimental.pallas import tpu as pltpu

F32 = jnp.float32
BF16 = jnp.bfloat16

D_MODEL = 2048
GROUP_W = 512
HEAD_DIM = 64
N_HEADS = 8
N_KV = 2
N_REP = 4
KV_W = N_KV * HEAD_DIM
SCONV_K = 3
CONF_K = 31
WINDOW = 128
GRID_W = 64
ROPE_THETA = 10000.0
D_FF = 4 * D_MODEL
IN_W = 4096
RMS_EPS = 1e-6
LN_EPS = 1e-5
NEG_BIG = -1e30
LOG2_E = math.log2(math.e)

LANES = 128
BF16_SUBLANES = 16
VMEM_LIMIT_BYTES = 56 * 1024 * 1024

COL_AB, COL_AC, COL_AV, COL_DA, COL_DG, COL_QB, COL_QC = (0, 512, 1024, 1536, 2048, 2560, 3072)
COL_KB, COL_VB, COL_KC, COL_VC = (3584, 3712, 3840, 3968)
_ORIG = dict(ab=0, ac=512, av=1024, qb=1536, kb=2048, vb=2176, qc=2304, kc=2816, vc=2944, da=3072, dg=3584)

IN_TN = 1024
ATT_TQ = 256
ATT_TK = 256
ATT_M = N_REP * ATT_TQ
WIN_TK = ATT_TQ + 2 * WINDOW
V_ROWS = HEAD_DIM + BF16_SUBLANES
MIX_TM = 512
MIX_CH = 64
HALO = 16
OUT_TM = 512
FFN_TM = 512
FFN_TF = 1024


def _cparams(sem):
    return pltpu.CompilerParams(dimension_semantics=sem, vmem_limit_bytes=VMEM_LIMIT_BYTES)


def _rope(x, cos, sin_signed, half, use_fwd):
    partner = jnp.where(use_fwd, pltpu.roll(x, half, 1), pltpu.roll(x, LANES - half, 1))
    return x * cos + partner * sin_signed


def _head_rms(x, gain, ones_bd):
    sq = x * x
    hi = sq.astype(BF16)
    lo = (sq - hi.astype(F32)).astype(BF16)
    ssum = (jnp.dot(hi, ones_bd, preferred_element_type=F32)
            + jnp.dot(lo, ones_bd, preferred_element_type=F32))
    return (x * lax.rsqrt(ssum * (1.0 / HEAD_DIM) + RMS_EPS)) * gain


def _in_proj_kernel(x_ref, g_ref, w_ref, cax_ref, sax_ref, cli_ref, sli_ref, qg_ref, kg_ref,
                    ones_ref, z_ref, h_ref):
    j = pl.program_id(1)

    @pl.when(j == 0)
    def _():
        x = x_ref[...]
        ms = jnp.mean(x * x, axis=-1, keepdims=True)
        h_ref[...] = ((x * lax.rsqrt(ms + RMS_EPS)) * g_ref[...]).astype(BF16)

    acc = jnp.dot(h_ref[...], w_ref[...], preferred_element_type=F32)

    @pl.when(j < 2)
    def _():
        z_ref[...] = acc.astype(BF16)

    lane = lax.broadcasted_iota(jnp.int32, (1, LANES), 1)
    fwd_ax = (lane % 32) >= 16
    fwd_li = (lane % 64) >= 32

    @pl.when(j == 2)
    def _():
        z_ref[:, 0:512] = acc[:, 0:512].astype(BF16)
        cax, sax, ones_bd, qg = cax_ref[...], sax_ref[...], ones_ref[...], qg_ref[...]
        for c in range(4):
            lo = 512 + c * LANES
            x = _head_rms(acc[:, lo:lo + LANES], qg, ones_bd)
            x = _rope(x, cax, sax, 16, fwd_ax) * (LOG2_E / math.sqrt(HEAD_DIM))
            z_ref[:, lo:lo + LANES] = x.astype(BF16)

    @pl.when(j == 3)
    def _():
        cax, sax, ones_bd, kg = cax_ref[...], sax_ref[...], ones_ref[...], kg_ref[...]
        cli, sli = cli_ref[...], sli_ref[...]
        for c in range(4):
            lo = c * LANES
            x = _rope(acc[:, lo:lo + LANES], cli, sli, 32, fwd_li) * (1.0 / math.sqrt(HEAD_DIM))
            z_ref[:, lo:lo + LANES] = x.astype(BF16)
        kb = _rope(_head_rms(acc[:, 512:640], kg, ones_bd), cax, sax, 16, fwd_ax)
        z_ref[:, 512:640] = kb.astype(BF16)
        z_ref[:, 640:768] = acc[:, 640:768].astype(BF16)
        z_ref[:, 768:896] = _rope(acc[:, 768:896], cli, sli, 32, fwd_li).astype(BF16)
        z_ref[:, 896:1024] = acc[:, 896:1024].astype(BF16)


def _in_proj(x2d, seq, g, w_bf, tabs, qg, kg, ones_bd):
    t = x2d.shape[0]
    tm = min(1024, seq)
    nseq = seq // tm
    row_spec = lambda width: pl.BlockSpec((tm, width), lambda i, j: (i % nseq, 0))
    const = lambda shape: pl.BlockSpec(shape, lambda i, j: (0, 0))
    return pl.pallas_call(
        _in_proj_kernel,
        grid=(t // tm, IN_W // IN_TN),
        in_specs=[
            pl.BlockSpec((tm, D_MODEL), lambda i, j: (i, 0)),
            const((1, D_MODEL)),
            pl.BlockSpec((D_MODEL, IN_TN), lambda i, j: (0, j)),
            row_spec(LANES), row_spec(LANES), row_spec(LANES), row_spec(LANES),
            const((1, LANES)), const((1, LANES)), const((LANES, LANES)),
        ],
        out_specs=pl.BlockSpec((tm, IN_TN), lambda i, j: (i, j)),
        out_shape=jax.ShapeDtypeStruct((t, IN_W), BF16),
        scratch_shapes=[pltpu.VMEM((tm, D_MODEL), BF16)],
        compiler_params=_cparams(("arbitrary", "arbitrary")),
        name="in_proj",
    )(x2d, g, w_bf, tabs[0], tabs[1], tabs[2], tabs[3], qg, kg, ones_bd)


def _build_qpad(q_ref, qpad_ref):
    qt = q_ref[...].astype(F32).T
    zeros = jnp.zeros((HEAD_DIM, ATT_M), BF16)
    for kvh in range(N_KV):
        qpad_ref[kvh, (1 - kvh) * HEAD_DIM:(2 - kvh) * HEAD_DIM, :] = zeros
        for g in range(N_REP):
            h = N_REP * kvh + g
            qpad_ref[kvh, kvh * HEAD_DIM:(kvh + 1) * HEAD_DIM, g * ATT_TQ:(g + 1) * ATT_TQ] = (
                qt[h * HEAD_DIM:(h + 1) * HEAD_DIM, :].astype(BF16))


def _store_heads(o_ref, ot_ref, o_by_kvh):
    for kvh in range(N_KV):
        for g in range(N_REP):
            h = N_REP * kvh + g
            ot_ref[h * HEAD_DIM:(h + 1) * HEAD_DIM, :] = o_by_kvh[kvh][:, g * ATT_TQ:(g + 1) * ATT_TQ]
    o_ref[...] = ot_ref[...].T.astype(BF16)


def _global_attn_kernel(q_ref, k_ref, v_ref, o_ref, qpad_ref, vt_ref, m_ref, acc_ref, ot_ref, s_ref, bm_ref,
                        *, nkb):
    i = pl.program_id(1)

    @pl.when(i == 0)
    def _():
        ones = jnp.ones((BF16_SUBLANES, ATT_TK), BF16)

        def fill(c, carry):
            r0 = pl.multiple_of(c * ATT_TK, ATT_TK)
            vt = v_ref[pl.ds(r0, ATT_TK), :].astype(F32).T
            for kvh in range(N_KV):
                vt_ref[kvh, c, 0:HEAD_DIM, :] = vt[kvh * HEAD_DIM:(kvh + 1) * HEAD_DIM, :].astype(BF16)
                vt_ref[kvh, c, HEAD_DIM:V_ROWS, :] = ones
            return carry

        lax.fori_loop(0, nkb, fill, 0)

    _build_qpad(q_ref, qpad_ref)
    m_ref[...] = jnp.full(m_ref.shape, NEG_BIG, F32)
    acc_ref[...] = jnp.zeros(acc_ref.shape, F32)

    def scores(c, slot):
        r0 = pl.multiple_of(c * ATT_TK, ATT_TK)
        kb = k_ref[pl.ds(r0, ATT_TK), :]
        for kvh in range(N_KV):
            s = jnp.dot(kb, qpad_ref[kvh], preferred_element_type=F32)
            s_ref[slot, kvh] = s
            bm_ref[slot, kvh] = jnp.max(s, axis=0, keepdims=True)

    def softmax_pv(c, slot):
        for kvh in range(N_KV):
            m_old = m_ref[kvh]
            m_new = jnp.maximum(m_old, bm_ref[slot, kvh])
            alpha = jnp.exp2(m_old - m_new)
            p = jnp.exp2(s_ref[slot, kvh] - m_new).astype(BF16)
            pv = jnp.dot(vt_ref[kvh, c], p, preferred_element_type=F32)
            acc_ref[kvh] = acc_ref[kvh] * alpha + pv
            m_ref[kvh] = m_new

    scores(0, 0)

    def body(it, carry):
        c = 2 * it
        scores(c + 1, 1)
        softmax_pv(c, 0)
        scores(jnp.minimum(c + 2, nkb - 1), 0)
        softmax_pv(c + 1, 1)
        return carry

    lax.fori_loop(0, nkb // 2, body, 0)

    outs = []
    for kvh in range(N_KV):
        acc = acc_ref[kvh]
        outs.append(acc[0:HEAD_DIM, :] / acc[HEAD_DIM:HEAD_DIM + 1, :])
    _store_heads(o_ref, ot_ref, outs)


def _global_attn(z, batch, seq):
    nq = seq // ATT_TQ
    nkb = seq // ATT_TK
    return pl.pallas_call(
        functools.partial(_global_attn_kernel, nkb=nkb),
        grid=(batch, nq),
        in_specs=[
            pl.BlockSpec((ATT_TQ, GROUP_W), lambda b, i: (b * nq + i, COL_QB // GROUP_W)),
            pl.BlockSpec((seq, KV_W), lambda b, i: (b, COL_KB // KV_W)),
            pl.BlockSpec((seq, KV_W), lambda b, i: (b, COL_VB // KV_W)),
        ],
        out_specs=pl.BlockSpec((ATT_TQ, GROUP_W), lambda b, i: (b * nq + i, 0)),
        out_shape=jax.ShapeDtypeStruct((batch * seq, GROUP_W), BF16),
        scratch_shapes=[
            pltpu.VMEM((N_KV, KV_W, ATT_M), BF16),
            pltpu.VMEM((N_KV, nkb, V_ROWS, ATT_TK), BF16),
            pltpu.VMEM((N_KV, 1, ATT_M), F32),
            pltpu.VMEM((N_KV, V_ROWS, ATT_M), F32),
            pltpu.VMEM((GROUP_W, ATT_TQ), F32),
            pltpu.VMEM((2, N_KV, ATT_TK, ATT_M), F32),
            pltpu.VMEM((2, N_KV, 1, ATT_M), F32),
        ],
        compiler_params=_cparams(("arbitrary", "arbitrary")),
        name="global_attn",
    )(z, z, z)


def _window_attn_kernel(q_ref, k_ref, v_ref, sink_ref, o_ref, qpad_ref, ot_ref, *, seq):
    i = pl.program_id(1)
    _build_qpad(q_ref, qpad_ref)
    q0 = i * ATT_TQ
    start = pl.multiple_of(jnp.clip(q0 - WINDOW, 0, seq - WIN_TK), WINDOW)
    kb = k_ref[pl.ds(start, WIN_TK), :]
    vt = v_ref[pl.ds(start, WIN_TK), :].astype(F32).T
    kpos = start + lax.broadcasted_iota(jnp.int32, (WIN_TK, ATT_M), 0)
    qpos = q0 + lax.broadcasted_iota(jnp.int32, (WIN_TK, ATT_M), 1) % ATT_TQ
    valid = jnp.abs(qpos - kpos) <= WINDOW
    outs = []
    for kvh in range(N_KV):
        s = jnp.dot(kb, qpad_ref[kvh], preferred_element_type=F32)
        s = jnp.where(valid, s, NEG_BIG)
        sk = sink_ref[kvh]
        m = jnp.maximum(jnp.max(s, axis=0, keepdims=True), sk)
        e = jnp.exp(s - m)
        denom = jnp.sum(e, axis=0, keepdims=True) + jnp.exp(sk - m)
        p = (e / denom).astype(BF16)
        vk = vt[kvh * HEAD_DIM:(kvh + 1) * HEAD_DIM, :].astype(BF16)
        outs.append(jnp.dot(vk, p, preferred_element_type=F32))
    _store_heads(o_ref, ot_ref, outs)


def _window_attn(z, sink_rows, batch, seq):
    nq = seq // ATT_TQ
    return pl.pallas_call(
        functools.partial(_window_attn_kernel, seq=seq),
        grid=(batch, nq),
        in_specs=[
            pl.BlockSpec((ATT_TQ, GROUP_W), lambda b, i: (b * nq + i, COL_QC // GROUP_W)),
            pl.BlockSpec((seq, KV_W), lambda b, i: (b, COL_KC // KV_W)),
            pl.BlockSpec((seq, KV_W), lambda b, i: (b, COL_VC // KV_W)),
            pl.BlockSpec((N_KV, 1, ATT_M), lambda b, i: (0, 0, 0)),
        ],
        out_specs=pl.BlockSpec((ATT_TQ, GROUP_W), lambda b, i: (b * nq + i, 0)),
        out_shape=jax.ShapeDtypeStruct((batch * seq, GROUP_W), BF16),
        scratch_shapes=[
            pltpu.VMEM((N_KV, KV_W, ATT_M), BF16),
            pltpu.VMEM((GROUP_W, ATT_TQ), F32),
        ],
        compiler_params=_cparams(("arbitrary", "arbitrary")),
        name="window_attn",
    )(z, z, z, sink_rows)


def _mixer_kernel(ab_ref, ac_ref, acp_ref, acn_ref, av_ref, avp_ref, avn_ref,
                  da_ref, dap_ref, dan_ref, dg_ref, dgp_ref, dgn_ref,
                  wa_ref, wd_ref, bd_ref, lng_ref, lnb_ref,
                  ya_ref, yd_ref, pbuf, ubuf, *, nseq):
    i = pl.program_id(0)
    keep_prev = jnp.where(i % nseq == 0, 0.0, 1.0)
    keep_next = jnp.where(i % nseq == nseq - 1, 0.0, 1.0)

    def gate(a_ref, g_ref):
        return a_ref[...].astype(F32) * jax.nn.sigmoid(g_ref[...].astype(F32))

    def prod(a_ref, b_ref):
        return a_ref[...].astype(F32) * b_ref[...].astype(F32)

    tm = MIX_TM
    ubuf[0:HALO, :] = gate(dap_ref, dgp_ref) * keep_prev
    ubuf[HALO:HALO + tm, :] = gate(da_ref, dg_ref)
    ubuf[HALO + tm:2 * HALO + tm, :] = gate(dan_ref, dgn_ref) * keep_next
    pbuf[0:HALO, :] = prod(acp_ref, avp_ref) * keep_prev
    pbuf[HALO:HALO + tm, :] = prod(ac_ref, av_ref)
    pbuf[HALO + tm:2 * HALO + tm, :] = prod(acn_ref, avn_ref) * keep_next

    bd, lng, lnb = bd_ref[...], lng_ref[...], lnb_ref[...]

    for c in range(tm // MIX_CH):
        r0 = c * MIX_CH
        acc = jnp.zeros((MIX_CH, GROUP_W), F32)
        for k in range(CONF_K):
            lo = r0 + HALO - CONF_K // 2 + k
            acc = acc + wd_ref[k:k + 1, :] * ubuf[lo:lo + MIX_CH, :]
        acc = acc + bd
        mu = jnp.mean(acc, axis=-1, keepdims=True)
        xc = acc - mu
        var = jnp.mean(xc * xc, axis=-1, keepdims=True)
        y = (xc * lax.rsqrt(var + LN_EPS)) * lng + lnb
        yd_ref[r0:r0 + MIX_CH, :] = (y * jax.nn.sigmoid(y)).astype(BF16)
        pa = jnp.zeros((MIX_CH, GROUP_W), F32)
        for k in range(SCONV_K):
            lo = r0 + HALO - SCONV_K // 2 + k
            pa = pa + wa_ref[k:k + 1, :] * pbuf[lo:lo + MIX_CH, :]
        ya_ref[r0:r0 + MIX_CH, :] = (ab_ref[r0:r0 + MIX_CH, :].astype(F32) * pa).astype(BF16)


def _mixer(z, seq, wa, wd, bd, lng, lnb):
    t = z.shape[0]
    tm = MIX_TM
    nseq = seq // tm
    hb = tm // HALO
    nh = t // HALO

    def cur(col):
        return pl.BlockSpec((tm, GROUP_W), lambda i: (i, col // GROUP_W))

    def prev(col):
        return pl.BlockSpec((HALO, GROUP_W), lambda i: (jnp.maximum(i * hb - 1, 0), col // GROUP_W))

    def nxt(col):
        return pl.BlockSpec((HALO, GROUP_W), lambda i: (jnp.minimum((i + 1) * hb, nh - 1), col // GROUP_W))

    const = lambda shape: pl.BlockSpec(shape, lambda i: (0, 0))
    halo_cols = (COL_AC, COL_AV, COL_DA, COL_DG)
    in_specs = [cur(COL_AB)]
    for col in halo_cols:
        in_specs += [cur(col), prev(col), nxt(col)]
    in_specs += [const((SCONV_K, GROUP_W)), const((CONF_K, GROUP_W)), const((1, GROUP_W)),
                 const((1, GROUP_W)), const((1, GROUP_W))]
    out_spec = pl.BlockSpec((tm, GROUP_W), lambda i: (i, 0))
    return pl.pallas_call(
        functools.partial(_mixer_kernel, nseq=nseq),
        grid=(t // tm,),
        in_specs=in_specs,
        out_specs=[out_spec, out_spec],
        out_shape=[jax.ShapeDtypeStruct((t, GROUP_W), BF16)] * 2,
        scratch_shapes=[pltpu.VMEM((tm + 2 * HALO, GROUP_W), F32),
                        pltpu.VMEM((tm + 2 * HALO, GROUP_W), F32)],
        compiler_params=_cparams(("arbitrary",)),
        name="mixer",
    )(*([z] * 13), wa, wd, bd, lng, lnb)


def _out_proj_kernel(ya_ref, yb_ref, yc_ref, yd_ref, w_ref, x_ref, g_ref, o_ref):
    y = jnp.concatenate([ya_ref[...], yb_ref[...], yc_ref[...], yd_ref[...]], axis=-1)
    acc = jnp.dot(y, w_ref[...], preferred_element_type=F32)
    ms = jnp.mean(acc * acc, axis=-1, keepdims=True)
    o_ref[...] = x_ref[...] + (acc * lax.rsqrt(ms + RMS_EPS)) * g_ref[...]


def _out_proj(ya, yb, yc, yd, w_bf, x2d, g):
    t = x2d.shape[0]
    tm = OUT_TM
    yspec = pl.BlockSpec((tm, GROUP_W), lambda i: (i, 0))
    xspec = pl.BlockSpec((tm, D_MODEL), lambda i: (i, 0))
    return pl.pallas_call(
        _out_proj_kernel,
        grid=(t // tm,),
        in_specs=[yspec, yspec, yspec, yspec,
                  pl.BlockSpec((D_MODEL, D_MODEL), lambda i: (0, 0)),
                  xspec,
                  pl.BlockSpec((1, D_MODEL), lambda i: (0, 0))],
        out_specs=xspec,
        out_shape=jax.ShapeDtypeStruct((t, D_MODEL), F32),
        compiler_params=_cparams(("arbitrary",)),
        name="out_proj",
    )(ya, yb, yc, yd, w_bf, x2d, g)


def _ffn_kernel(x_ref, g1_ref, w1_ref, w2_ref, g2_ref, o_ref, h_ref, acc_ref, *, nf):
    j = pl.program_id(1)

    @pl.when(j == 0)
    def _():
        x = x_ref[...]
        ms = jnp.mean(x * x, axis=-1, keepdims=True)
        h_ref[...] = ((x * lax.rsqrt(ms + RMS_EPS)) * g1_ref[...]).astype(BF16)
        acc_ref[...] = jnp.zeros(acc_ref.shape, F32)

    u = jnp.dot(h_ref[...], w1_ref[...], preferred_element_type=F32)
    u = jnp.maximum(u, 0.0)
    u = (u * u).astype(BF16)
    acc_ref[...] += jnp.dot(u, w2_ref[...], preferred_element_type=F32)

    @pl.when(j == nf - 1)
    def _():
        y = acc_ref[...]
        ms = jnp.mean(y * y, axis=-1, keepdims=True)
        o_ref[...] = x_ref[...] + (y * lax.rsqrt(ms + RMS_EPS)) * g2_ref[...]


def _ffn(x2d, g1, w1_bf, w2_bf, g2):
    t = x2d.shape[0]
    tm, tf = FFN_TM, FFN_TF
    nf = D_FF // tf
    xspec = pl.BlockSpec((tm, D_MODEL), lambda i, j: (i, 0))
    gspec = pl.BlockSpec((1, D_MODEL), lambda i, j: (0, 0))
    return pl.pallas_call(
        functools.partial(_ffn_kernel, nf=nf),
        grid=(t // tm, nf),
        in_specs=[xspec, gspec,
                  pl.BlockSpec((D_MODEL, tf), lambda i, j: (0, j)),
                  pl.BlockSpec((tf, D_MODEL), lambda i, j: (j, 0)),
                  gspec],
        out_specs=xspec,
        out_shape=jax.ShapeDtypeStruct((t, D_MODEL), F32),
        scratch_shapes=[pltpu.VMEM((tm, D_MODEL), BF16), pltpu.VMEM((tm, D_MODEL), F32)],
        compiler_params=_cparams(("arbitrary", "arbitrary")),
        name="ffn",
    )(x2d, g1, w1_bf, w2_bf, g2)


def _rope_tables(seq):
    lane = np.arange(LANES)
    t = jnp.arange(seq)
    inv_ax = 1.0 / (ROPE_THETA ** (jnp.arange(0, 32, 2, dtype=F32) / 32))
    pos_row = (t // GRID_W).astype(F32)
    pos_col = (t % GRID_W).astype(F32)
    use_col = ((lane % HEAD_DIM) >= 32)
    pos = jnp.where(use_col[None, :], pos_col[:, None], pos_row[:, None])
    ang = pos * inv_ax[lane % 16][None, :]
    sign_ax = np.where((lane % 32) >= 16, 1.0, -1.0).astype(np.float32)
    cax, sax = jnp.cos(ang), jnp.sin(ang) * sign_ax[None, :]
    inv_li = 1.0 / (ROPE_THETA ** (jnp.arange(0, HEAD_DIM, 2, dtype=F32) / HEAD_DIM))
    ang = t.astype(F32)[:, None] * inv_li[lane % 32][None, :]
    sign_li = np.where((lane % HEAD_DIM) >= 32, 1.0, -1.0).astype(np.float32)
    cli, sli = jnp.cos(ang), jnp.sin(ang) * sign_li[None, :]
    return cax, sax, cli, sli


def _permute_w_in(w):
    o = _ORIG
    order = [("ab", 512), ("ac", 512), ("av", 512), ("da", 512), ("dg", 512), ("qb", 512), ("qc", 512),
             ("kb", 128), ("vb", 128), ("kc", 128), ("vc", 128)]
    return jnp.concatenate([w[:, o[n]:o[n] + wd] for n, wd in order], axis=1).astype(BF16)


def _layer(x2d, batch, seq, tabs, ones_bd, p):
    z = _in_proj(x2d, seq, p["pre_mix_g"], p["w_in"], tabs, p["q_norm_g"], p["k_norm_g"], ones_bd)
    yb = _global_attn(z, batch, seq)
    yc = _window_attn(z, p["sink_rows"], batch, seq)
    ya, yd = _mixer(z, seq, p["conv_a_w"], p["conv_d_w"], p["conv_d_b"], p["ln_d_g"], p["ln_d_b"])
    x1 = _out_proj(ya, yb, yc, yd, p["w_out"], x2d, p["post_mix_g"])
    return _ffn(x1, p["pre_ffn_g"], p["w_ff1"], p["w_ff2"], p["post_ffn_g"])


def kernel(x_prompt, x_sample, pre_mix_g, w_in, conv_a_w, q_norm_g, k_norm_g, sink_c, conv_d_w, conv_d_b,
           ln_d_g, ln_d_b, w_out, post_mix_g, pre_ffn_g, w_ff1, w_ff2, post_ffn_g):
    depth = w_in.shape[0]
    lane = np.arange(LANES)
    ones_bd = jnp.asarray((lane[:, None] // HEAD_DIM) == (lane[None, :] // HEAD_DIM), BF16)
    layers = []
    for l in range(depth):
        row = lambda a: a[l].reshape(1, -1).astype(F32)
        sink_rows = jnp.repeat(sink_c[l].astype(F32).reshape(N_KV, 1, N_REP), ATT_TQ, axis=2)
        layers.append(dict(
            pre_mix_g=row(pre_mix_g), w_in=_permute_w_in(w_in[l]),
            conv_a_w=conv_a_w[l].astype(F32), conv_d_w=conv_d_w[l].astype(F32),
            conv_d_b=row(conv_d_b), ln_d_g=row(ln_d_g), ln_d_b=row(ln_d_b),
            q_norm_g=jnp.tile(row(q_norm_g), (1, 2)), k_norm_g=jnp.tile(row(k_norm_g), (1, 2)),
            sink_rows=sink_rows, w_out=w_out[l].astype(BF16), post_mix_g=row(post_mix_g),
            pre_ffn_g=row(pre_ffn_g), w_ff1=w_ff1[l].astype(BF16), w_ff2=w_ff2[l].astype(BF16),
            post_ffn_g=row(post_ffn_g)))
    outs = []
    for x in (x_prompt, x_sample):
        batch, seq, _ = x.shape
        tabs = _rope_tables(seq)
        y = x.reshape(batch * seq, D_MODEL)
        for p in layers:
            y = _layer(y, batch, seq, tabs, ones_bd, p)
        outs.append(y.reshape(batch, seq, D_MODEL))
    return tuple(outs)
```

```python
import functools
import math

import jax
import jax.numpy as jnp
import numpy as np
from jax import lax
from jax.experimental import pallas as pl
from jax.experimental.pallas import tpu as pltpu

F32 = jnp.float32
BF16 = jnp.bfloat16

D_MODEL = 2048
GROUP_W = 512
HEAD_DIM = 64
N_HEADS = 8
N_KV = 2
N_REP = 4
KV_W = N_KV * HEAD_DIM
SCONV_K = 3
CONF_K = 31
WINDOW = 128
GRID_W = 64
ROPE_THETA = 10000.0
D_FF = 4 * D_MODEL
IN_W = 4096
RMS_EPS = 1e-6
LN_EPS = 1e-5
NEG_BIG = -1e30
LOG2_E = math.log2(math.e)

LANES = 128
BF16_SUBLANES = 16
F32_SUBLANES = 8
VMEM_LIMIT_BYTES = 56 * 1024 * 1024

COL_AB, COL_AC, COL_AV, COL_DA, COL_DG, COL_QB, COL_QC = (0, 512, 1024, 1536, 2048, 2560, 3072)
COL_KB, COL_VB, COL_KC, COL_VC = (3584, 3712, 3840, 3968)
_ORIG = dict(ab=0, ac=512, av=1024, qb=1536, kb=2048, vb=2176, qc=2304, kc=2816, vc=2944, da=3072, dg=3584)

IN_TN = 1024
ATT_TQ = 256
ATT_TK = 512
ATT_M = N_REP * ATT_TQ
ATT_CW = 256
WIN_TK = ATT_TQ + 2 * WINDOW
V_ROWS = HEAD_DIM + BF16_SUBLANES
MIX_TM = 512
MIX_CH = 64
HALO = 16
OUT_TM = 512
FFN_TM = 512
FFN_TF = 1024


def _cparams(sem, flags=None):
    return pltpu.CompilerParams(dimension_semantics=sem, vmem_limit_bytes=VMEM_LIMIT_BYTES, flags=flags)


def _rope(x, cos, sin_signed, half, use_fwd):
    partner = jnp.where(use_fwd, pltpu.roll(x, half, 1), pltpu.roll(x, LANES - half, 1))
    return x * cos + partner * sin_signed


def _head_rms(x, gain, ones_bd):
    sq = x * x
    hi = sq.astype(BF16)
    lo = (sq - hi.astype(F32)).astype(BF16)
    ssum = (jnp.dot(hi, ones_bd, preferred_element_type=F32)
            + jnp.dot(lo, ones_bd, preferred_element_type=F32))
    return (x * lax.rsqrt(ssum * (1.0 / HEAD_DIM) + RMS_EPS)) * gain


def _in_proj_kernel(x_ref, g_ref, w_ref, cax_ref, sax_ref, cli_ref, sli_ref, qg_ref, kg_ref,
                    ones_ref, z_ref, h_ref):
    j = pl.program_id(1)

    @pl.when(j == 0)
    def _():
        x = x_ref[...]
        ms = jnp.mean(x * x, axis=-1, keepdims=True)
        h_ref[...] = ((x * lax.rsqrt(ms + RMS_EPS)) * g_ref[...]).astype(BF16)

    acc = jnp.dot(h_ref[...], w_ref[...], preferred_element_type=F32)

    @pl.when(j < 2)
    def _():
        z_ref[...] = acc.astype(BF16)

    lane = lax.broadcasted_iota(jnp.int32, (1, LANES), 1)
    fwd_ax = (lane % 32) >= 16
    fwd_li = (lane % 64) >= 32

    @pl.when(j == 2)
    def _():
        z_ref[:, 0:512] = acc[:, 0:512].astype(BF16)
        cax, sax, ones_bd, qg = cax_ref[...], sax_ref[...], ones_ref[...], qg_ref[...]
        for c in range(4):
            lo = 512 + c * LANES
            x = _head_rms(acc[:, lo:lo + LANES], qg, ones_bd)
            x = _rope(x, cax, sax, 16, fwd_ax) * (LOG2_E / math.sqrt(HEAD_DIM))
            z_ref[:, lo:lo + LANES] = x.astype(BF16)

    @pl.when(j == 3)
    def _():
        cax, sax, ones_bd, kg = cax_ref[...], sax_ref[...], ones_ref[...], kg_ref[...]
        cli, sli = cli_ref[...], sli_ref[...]
        for c in range(4):
            lo = c * LANES
            x = _rope(acc[:, lo:lo + LANES], cli, sli, 32, fwd_li) * (1.0 / math.sqrt(HEAD_DIM))
            z_ref[:, lo:lo + LANES] = x.astype(BF16)
        kb = _rope(_head_rms(acc[:, 512:640], kg, ones_bd), cax, sax, 16, fwd_ax)
        z_ref[:, 512:640] = kb.astype(BF16)
        z_ref[:, 640:768] = acc[:, 640:768].astype(BF16)
        z_ref[:, 768:896] = _rope(acc[:, 768:896], cli, sli, 32, fwd_li).astype(BF16)
        z_ref[:, 896:1024] = acc[:, 896:1024].astype(BF16)


def _in_proj(x2d, seq, g, w_bf, tabs, qg, kg, ones_bd):
    t = x2d.shape[0]
    tm = min(1024, seq)
    nseq = seq // tm
    row_spec = lambda width: pl.BlockSpec((tm, width), lambda i, j: (i % nseq, 0))
    const = lambda shape: pl.BlockSpec(shape, lambda i, j: (0, 0))
    return pl.pallas_call(
        _in_proj_kernel,
        grid=(t // tm, IN_W // IN_TN),
        in_specs=[
            pl.BlockSpec((tm, D_MODEL), lambda i, j: (i, 0)),
            const((1, D_MODEL)),
            pl.BlockSpec((D_MODEL, IN_TN), lambda i, j: (0, j)),
            row_spec(LANES), row_spec(LANES), row_spec(LANES), row_spec(LANES),
            const((1, LANES)), const((1, LANES)), const((LANES, LANES)),
        ],
        out_specs=pl.BlockSpec((tm, IN_TN), lambda i, j: (i, j)),
        out_shape=jax.ShapeDtypeStruct((t, IN_W), BF16),
        scratch_shapes=[pltpu.VMEM((tm, D_MODEL), BF16)],
        compiler_params=_cparams(("arbitrary", "arbitrary")),
        name="in_proj",
    )(x2d, g, w_bf, tabs[0], tabs[1], tabs[2], tabs[3], qg, kg, ones_bd)


def _build_qpad(q_ref, qpad_ref):
    qt = q_ref[...].astype(F32).T
    zeros = jnp.zeros((HEAD_DIM, ATT_M), BF16)
    for kvh in range(N_KV):
        qpad_ref[kvh, (1 - kvh) * HEAD_DIM:(2 - kvh) * HEAD_DIM, :] = zeros
        for g in range(N_REP):
            h = N_REP * kvh + g
            qpad_ref[kvh, kvh * HEAD_DIM:(kvh + 1) * HEAD_DIM, g * ATT_TQ:(g + 1) * ATT_TQ] = (
                qt[h * HEAD_DIM:(h + 1) * HEAD_DIM, :].astype(BF16))


def _store_heads(o_ref, ot_ref, o_by_kvh):
    for kvh in range(N_KV):
        for g in range(N_REP):
            h = N_REP * kvh + g
            ot_ref[h * HEAD_DIM:(h + 1) * HEAD_DIM, :] = o_by_kvh[kvh][:, g * ATT_TQ:(g + 1) * ATT_TQ]
    o_ref[...] = ot_ref[...].T.astype(BF16)


def _global_attn_kernel(q_ref, k_ref, v_ref, o_ref, qpad_ref, vt_ref, m_ref, acc_ref, ot_ref, s_ref, bm_ref,
                        *, nkb):
    i = pl.program_id(1)

    @pl.when(i == 0)
    def _():
        ones = jnp.ones((BF16_SUBLANES, ATT_TK), BF16)

        def fill(c, carry):
            r0 = pl.multiple_of(c * ATT_TK, ATT_TK)
            vt = v_ref[pl.ds(r0, ATT_TK), :].astype(F32).T
            for kvh in range(N_KV):
                vt_ref[kvh, c, 0:HEAD_DIM, :] = vt[kvh * HEAD_DIM:(kvh + 1) * HEAD_DIM, :].astype(BF16)
                vt_ref[kvh, c, HEAD_DIM:V_ROWS, :] = ones
            return carry

        lax.fori_loop(0, nkb, fill, 0)

    _build_qpad(q_ref, qpad_ref)
    m_ref[...] = jnp.full(m_ref.shape, NEG_BIG, F32)
    acc_ref[...] = jnp.zeros(acc_ref.shape, F32)

    chunks = [(kvh, slice(j * ATT_CW, (j + 1) * ATT_CW))
              for kvh in range(N_KV) for j in range(ATT_M // ATT_CW)]

    def load_keys(c):
        r0 = pl.multiple_of(c * ATT_TK, ATT_TK)
        return k_ref[pl.ds(r0, ATT_TK), :]

    def scores(kb, slot, kvh, cols):
        s = jnp.dot(kb, qpad_ref[kvh, :, cols], preferred_element_type=F32)
        s_ref[slot, kvh, :, cols] = s
        bm_ref[slot, kvh, :, cols] = jnp.max(s, axis=0, keepdims=True)

    def softmax_pv(c, slot, kvh, cols):
        m_old = m_ref[kvh, :, cols]
        m_new = jnp.maximum(m_old, bm_ref[slot, kvh, :, cols])
        alpha = jnp.exp2(m_old - m_new)
        p = jnp.exp2(s_ref[slot, kvh, :, cols] - m_new).astype(BF16)
        pv = jnp.dot(vt_ref[kvh, c], p, preferred_element_type=F32)
        acc_ref[kvh, :, cols] = acc_ref[kvh, :, cols] * alpha + pv
        m_ref[kvh, :, cols] = m_new

    kb0 = load_keys(0)
    for kvh, cols in chunks:
        scores(kb0, 0, kvh, cols)

    def body(it, carry):
        c = 2 * it
        kb = load_keys(c + 1)
        for kvh, cols in chunks:
            scores(kb, 1, kvh, cols)
            softmax_pv(c, 0, kvh, cols)
        kb = load_keys(jnp.minimum(c + 2, nkb - 1))
        for kvh, cols in chunks:
            scores(kb, 0, kvh, cols)
            softmax_pv(c + 1, 1, kvh, cols)
        return carry

    lax.fori_loop(0, nkb // 2, body, 0)

    outs = []
    for kvh in range(N_KV):
        acc = acc_ref[kvh]
        outs.append(acc[0:HEAD_DIM, :] / acc[HEAD_DIM:HEAD_DIM + 1, :])
    _store_heads(o_ref, ot_ref, outs)


def _global_attn(z, batch, seq):
    nq = seq // ATT_TQ
    nkb = seq // ATT_TK
    return pl.pallas_call(
        functools.partial(_global_attn_kernel, nkb=nkb),
        grid=(batch, nq),
        in_specs=[
            pl.BlockSpec((ATT_TQ, GROUP_W), lambda b, i: (b * nq + i, COL_QB // GROUP_W)),
            pl.BlockSpec((seq, KV_W), lambda b, i: (b, COL_KB // KV_W)),
            pl.BlockSpec((seq, KV_W), lambda b, i: (b, COL_VB // KV_W)),
        ],
        out_specs=pl.BlockSpec((ATT_TQ, GROUP_W), lambda b, i: (b * nq + i, 0)),
        out_shape=jax.ShapeDtypeStruct((batch * seq, GROUP_W), BF16),
        scratch_shapes=[
            pltpu.VMEM((N_KV, KV_W, ATT_M), BF16),
            pltpu.VMEM((N_KV, nkb, V_ROWS, ATT_TK), BF16),
            pltpu.VMEM((N_KV, 1, ATT_M), F32),
            pltpu.VMEM((N_KV, V_ROWS, ATT_M), F32),
            pltpu.VMEM((GROUP_W, ATT_TQ), F32),
            pltpu.VMEM((2, N_KV, ATT_TK, ATT_M), F32),
            pltpu.VMEM((2, N_KV, 1, ATT_M), F32),
        ],
        compiler_params=_cparams(("arbitrary", "arbitrary")),
        name="global_attn",
    )(z, z, z)


def _window_attn_kernel(q_ref, k_ref, v_ref, sink_ref, o_ref, qpad_ref, ot_ref, *, seq):
    i = pl.program_id(1)
    _build_qpad(q_ref, qpad_ref)
    q0 = i * ATT_TQ
    start = pl.multiple_of(jnp.clip(q0 - WINDOW, 0, seq - WIN_TK), WINDOW)
    kb = k_ref[pl.ds(start, WIN_TK), :]
    vt = v_ref[pl.ds(start, WIN_TK), :].astype(F32).T
    kpos = start + lax.broadcasted_iota(jnp.int32, (WIN_TK, ATT_TQ), 0)
    qpos = q0 + lax.broadcasted_iota(jnp.int32, (WIN_TK, ATT_TQ), 1)
    valid = jnp.abs(qpos - kpos) <= WINDOW
    vks = [vt[kvh * HEAD_DIM:(kvh + 1) * HEAD_DIM, :].astype(BF16) for kvh in range(N_KV)]
    heads = [(kvh, g) for kvh in range(N_KV) for g in range(N_REP)]

    def scores(kvh, g):
        s = jnp.dot(kb, qpad_ref[kvh, :, g * ATT_TQ:(g + 1) * ATT_TQ], preferred_element_type=F32)
        return jnp.where(valid, s, NEG_BIG)

    def finish(kvh, g, s):
        sk = sink_ref[kvh, :, g * ATT_TQ:(g + 1) * ATT_TQ]
        m = jnp.maximum(jnp.max(s, axis=0, keepdims=True), sk)
        e = jnp.exp(s - m)
        denom = jnp.sum(e, axis=0, keepdims=True) + jnp.exp(sk - m)
        o = jnp.dot(vks[kvh], e.astype(BF16), preferred_element_type=F32)
        h = N_REP * kvh + g
        ot_ref[h * HEAD_DIM:(h + 1) * HEAD_DIM, :] = o / denom

    s_next = scores(*heads[0])
    for n, head in enumerate(heads):
        s_cur = s_next
        if n + 1 < len(heads):
            s_next = scores(*heads[n + 1])
        finish(*head, s_cur)
    o_ref[...] = ot_ref[...].T.astype(BF16)


def _window_attn(z, sink_rows, batch, seq):
    nq = seq // ATT_TQ
    return pl.pallas_call(
        functools.partial(_window_attn_kernel, seq=seq),
        grid=(batch, nq),
        in_specs=[
            pl.BlockSpec((ATT_TQ, GROUP_W), lambda b, i: (b * nq + i, COL_QC // GROUP_W)),
            pl.BlockSpec((seq, KV_W), lambda b, i: (b, COL_KC // KV_W)),
            pl.BlockSpec((seq, KV_W), lambda b, i: (b, COL_VC // KV_W)),
            pl.BlockSpec((N_KV, 1, ATT_M), lambda b, i: (0, 0, 0)),
        ],
        out_specs=pl.BlockSpec((ATT_TQ, GROUP_W), lambda b, i: (b * nq + i, 0)),
        out_shape=jax.ShapeDtypeStruct((batch * seq, GROUP_W), BF16),
        scratch_shapes=[
            pltpu.VMEM((N_KV, KV_W, ATT_M), BF16),
            pltpu.VMEM((GROUP_W, ATT_TQ), F32),
        ],
        compiler_params=_cparams(("arbitrary", "arbitrary")),
        name="window_attn",
    )(z, z, z, sink_rows)


def _mixer_kernel(ab_ref, ac_ref, acp_ref, acn_ref, av_ref, avp_ref, avn_ref,
                  da_ref, dap_ref, dan_ref, dg_ref, dgp_ref, dgn_ref,
                  wa_ref, wd_ref, bd_ref, lng_ref, lnb_ref,
                  ya_ref, yd_ref, pbuf, ubuf, ush, *, nseq):
    i = pl.program_id(0)
    keep_prev = jnp.where(i % nseq == 0, 0.0, 1.0)
    keep_next = jnp.where(i % nseq == nseq - 1, 0.0, 1.0)

    def gate(a_ref, g_ref):
        return a_ref[...].astype(F32) * jax.nn.sigmoid(g_ref[...].astype(F32))

    def prod(a_ref, b_ref):
        return a_ref[...].astype(F32) * b_ref[...].astype(F32)

    tm = MIX_TM
    ubuf[0:HALO, :] = gate(dap_ref, dgp_ref) * keep_prev
    ubuf[HALO:HALO + tm, :] = gate(da_ref, dg_ref)
    ubuf[HALO + tm:2 * HALO + tm, :] = gate(dan_ref, dgn_ref) * keep_next
    pbuf[0:HALO, :] = prod(acp_ref, avp_ref) * keep_prev
    pbuf[HALO:HALO + tm, :] = prod(ac_ref, av_ref)
    pbuf[HALO + tm:2 * HALO + tm, :] = prod(acn_ref, avn_ref) * keep_next

    rows = tm + 2 * HALO
    ubuf[rows:rows + F32_SUBLANES, :] = jnp.zeros((F32_SUBLANES, GROUP_W), F32)
    for s in range(1, F32_SUBLANES):
        ush[s - 1, :, :] = ubuf[s:s + rows, :]

    bd, lng, lnb = bd_ref[...], lng_ref[...], lnb_ref[...]

    for c in range(tm // MIX_CH):
        r0 = c * MIX_CH
        acc = jnp.zeros((MIX_CH, GROUP_W), F32)
        for k in range(CONF_K):
            lo = r0 + HALO - CONF_K // 2 + k
            s = lo % F32_SUBLANES
            a = lo - s
            tap = ubuf[a:a + MIX_CH, :] if s == 0 else ush[s - 1, a:a + MIX_CH, :]
            wk = wd_ref[k * F32_SUBLANES:(k + 1) * F32_SUBLANES, :]
            acc = acc + (tap.reshape(MIX_CH // F32_SUBLANES, F32_SUBLANES, GROUP_W) * wk[None]
                         ).reshape(MIX_CH, GROUP_W)
        acc = acc + bd
        mu = jnp.mean(acc, axis=-1, keepdims=True)
        xc = acc - mu
        var = jnp.mean(xc * xc, axis=-1, keepdims=True)
        y = (xc * lax.rsqrt(var + LN_EPS)) * lng + lnb
        yd_ref[r0:r0 + MIX_CH, :] = (y * jax.nn.sigmoid(y)).astype(BF16)
        pa = jnp.zeros((MIX_CH, GROUP_W), F32)
        for k in range(SCONV_K):
            lo = r0 + HALO - SCONV_K // 2 + k
            pa = pa + wa_ref[k:k + 1, :] * pbuf[lo:lo + MIX_CH, :]
        ya_ref[r0:r0 + MIX_CH, :] = (ab_ref[r0:r0 + MIX_CH, :].astype(F32) * pa).astype(BF16)


def _mixer(z, seq, wa, wd, bd, lng, lnb):
    t = z.shape[0]
    tm = MIX_TM
    nseq = seq // tm
    hb = tm // HALO
    nh = t // HALO

    def cur(col):
        return pl.BlockSpec((tm, GROUP_W), lambda i: (i, col // GROUP_W))

    def prev(col):
        return pl.BlockSpec((HALO, GROUP_W), lambda i: (jnp.maximum(i * hb - 1, 0), col // GROUP_W))

    def nxt(col):
        return pl.BlockSpec((HALO, GROUP_W), lambda i: (jnp.minimum((i + 1) * hb, nh - 1), col // GROUP_W))

    const = lambda shape: pl.BlockSpec(shape, lambda i: (0, 0))
    halo_cols = (COL_AC, COL_AV, COL_DA, COL_DG)
    in_specs = [cur(COL_AB)]
    for col in halo_cols:
        in_specs += [cur(col), prev(col), nxt(col)]
    in_specs += [const((SCONV_K, GROUP_W)), const((CONF_K * F32_SUBLANES, GROUP_W)), const((1, GROUP_W)),
                 const((1, GROUP_W)), const((1, GROUP_W))]
    out_spec = pl.BlockSpec((tm, GROUP_W), lambda i: (i, 0))
    return pl.pallas_call(
        functools.partial(_mixer_kernel, nseq=nseq),
        grid=(t // tm,),
        in_specs=in_specs,
        out_specs=[out_spec, out_spec],
        out_shape=[jax.ShapeDtypeStruct((t, GROUP_W), BF16)] * 2,
        scratch_shapes=[pltpu.VMEM((tm + 2 * HALO, GROUP_W), F32),
                        pltpu.VMEM((tm + 2 * HALO + F32_SUBLANES, GROUP_W), F32),
                        pltpu.VMEM((F32_SUBLANES - 1, tm + 2 * HALO, GROUP_W), F32)],
        compiler_params=_cparams(("arbitrary",)),
        name="mixer",
    )(*([z] * 13), wa, wd, bd, lng, lnb)


def _out_proj_kernel(ya_ref, yb_ref, yc_ref, yd_ref, w_ref, x_ref, g_ref, o_ref):
    y = jnp.concatenate([ya_ref[...], yb_ref[...], yc_ref[...], yd_ref[...]], axis=-1)
    acc = jnp.dot(y, w_ref[...], preferred_element_type=F32)
    ms = jnp.mean(acc * acc, axis=-1, keepdims=True)
    o_ref[...] = x_ref[...] + (acc * lax.rsqrt(ms + RMS_EPS)) * g_ref[...]


def _out_proj(ya, yb, yc, yd, w_bf, x2d, g):
    t = x2d.shape[0]
    tm = OUT_TM
    yspec = pl.BlockSpec((tm, GROUP_W), lambda i: (i, 0))
    xspec = pl.BlockSpec((tm, D_MODEL), lambda i: (i, 0))
    return pl.pallas_call(
        _out_proj_kernel,
        grid=(t // tm,),
        in_specs=[yspec, yspec, yspec, yspec,
                  pl.BlockSpec((D_MODEL, D_MODEL), lambda i: (0, 0)),
                  xspec,
                  pl.BlockSpec((1, D_MODEL), lambda i: (0, 0))],
        out_specs=xspec,
        out_shape=jax.ShapeDtypeStruct((t, D_MODEL), F32),
        compiler_params=_cparams(("arbitrary",)),
        name="out_proj",
    )(ya, yb, yc, yd, w_bf, x2d, g)


def _ffn_kernel(x_ref, g1_ref, w1_ref, w2_ref, g2_ref, o_ref, h_ref, acc_ref, *, nf):
    j = pl.program_id(1)

    @pl.when(j == 0)
    def _():
        x = x_ref[...]
        ms = jnp.mean(x * x, axis=-1, keepdims=True)
        h_ref[...] = ((x * lax.rsqrt(ms + RMS_EPS)) * g1_ref[...]).astype(BF16)
        acc_ref[...] = jnp.zeros(acc_ref.shape, F32)

    u = jnp.dot(h_ref[...], w1_ref[...], preferred_element_type=F32)
    u = jnp.maximum(u, 0.0)
    u = (u * u).astype(BF16)
    acc_ref[...] += jnp.dot(u, w2_ref[...], preferred_element_type=F32)

    @pl.when(j == nf - 1)
    def _():
        y = acc_ref[...]
        ms = jnp.mean(y * y, axis=-1, keepdims=True)
        o_ref[...] = x_ref[...] + (y * lax.rsqrt(ms + RMS_EPS)) * g2_ref[...]


def _ffn(x2d, g1, w1_bf, w2_bf, g2):
    t = x2d.shape[0]
    tm, tf = FFN_TM, FFN_TF
    nf = D_FF // tf
    xspec = pl.BlockSpec((tm, D_MODEL), lambda i, j: (i, 0))
    gspec = pl.BlockSpec((1, D_MODEL), lambda i, j: (0, 0))
    return pl.pallas_call(
        functools.partial(_ffn_kernel, nf=nf),
        grid=(t // tm, nf),
        in_specs=[xspec, gspec,
                  pl.BlockSpec((D_MODEL, tf), lambda i, j: (0, j)),
                  pl.BlockSpec((tf, D_MODEL), lambda i, j: (j, 0)),
                  gspec],
        out_specs=xspec,
        out_shape=jax.ShapeDtypeStruct((t, D_MODEL), F32),
        scratch_shapes=[pltpu.VMEM((tm, D_MODEL), BF16), pltpu.VMEM((tm, D_MODEL), F32)],
        compiler_params=_cparams(("arbitrary", "arbitrary")),
        name="ffn",
    )(x2d, g1, w1_bf, w2_bf, g2)


def _rope_tables(seq):
    lane = np.arange(LANES)
    t = jnp.arange(seq)
    inv_ax = 1.0 / (ROPE_THETA ** (jnp.arange(0, 32, 2, dtype=F32) / 32))
    pos_row = (t // GRID_W).astype(F32)
    pos_col = (t % GRID_W).astype(F32)
    use_col = ((lane % HEAD_DIM) >= 32)
    pos = jnp.where(use_col[None, :], pos_col[:, None], pos_row[:, None])
    ang = pos * inv_ax[lane % 16][None, :]
    sign_ax = np.where((lane % 32) >= 16, 1.0, -1.0).astype(np.float32)
    cax, sax = jnp.cos(ang), jnp.sin(ang) * sign_ax[None, :]
    inv_li = 1.0 / (ROPE_THETA ** (jnp.arange(0, HEAD_DIM, 2, dtype=F32) / HEAD_DIM))
    ang = t.astype(F32)[:, None] * inv_li[lane % 32][None, :]
    sign_li = np.where((lane % HEAD_DIM) >= 32, 1.0, -1.0).astype(np.float32)
    cli, sli = jnp.cos(ang), jnp.sin(ang) * sign_li[None, :]
    return cax, sax, cli, sli


def _permute_w_in(w):
    o = _ORIG
    order = [("ab", 512), ("ac", 512), ("av", 512), ("da", 512), ("dg", 512), ("qb", 512), ("qc", 512),
             ("kb", 128), ("vb", 128), ("kc", 128), ("vc", 128)]
    return jnp.concatenate([w[:, o[n]:o[n] + wd] for n, wd in order], axis=1).astype(BF16)


def _layer(x2d, batch, seq, tabs, ones_bd, p):
    z = _in_proj(x2d, seq, p["pre_mix_g"], p["w_in"], tabs, p["q_norm_g"], p["k_norm_g"], ones_bd)
    yb = _global_attn(z, batch, seq)
    yc = _window_attn(z, p["sink_rows"], batch, seq)
    ya, yd = _mixer(z, seq, p["conv_a_w"], p["conv_d_w"], p["conv_d_b"], p["ln_d_g"], p["ln_d_b"])
    x1 = _out_proj(ya, yb, yc, yd, p["w_out"], x2d, p["post_mix_g"])
    return _ffn(x1, p["pre_ffn_g"], p["w_ff1"], p["w_ff2"], p["post_ffn_g"])


def kernel(x_prompt, x_sample, pre_mix_g, w_in, conv_a_w, q_norm_g, k_norm_g, sink_c, conv_d_w, conv_d_b,
           ln_d_g, ln_d_b, w_out, post_mix_g, pre_ffn_g, w_ff1, w_ff2, post_ffn_g):
    depth = w_in.shape[0]
    lane = np.arange(LANES)
    ones_bd = jnp.asarray((lane[:, None] // HEAD_DIM) == (lane[None, :] // HEAD_DIM), BF16)
    layers = []
    for l in range(depth):
        row = lambda a: a[l].reshape(1, -1).astype(F32)
        sink_rows = jnp.repeat(sink_c[l].astype(F32).reshape(N_KV, 1, N_REP), ATT_TQ, axis=2)
        layers.append(dict(
            pre_mix_g=row(pre_mix_g), w_in=_permute_w_in(w_in[l]),
            conv_a_w=conv_a_w[l].astype(F32),
            conv_d_w=jnp.repeat(conv_d_w[l].astype(F32), F32_SUBLANES, axis=0),
            conv_d_b=row(conv_d_b), ln_d_g=row(ln_d_g), ln_d_b=row(ln_d_b),
            q_norm_g=jnp.tile(row(q_norm_g), (1, 2)), k_norm_g=jnp.tile(row(k_norm_g), (1, 2)),
            sink_rows=sink_rows, w_out=w_out[l].astype(BF16), post_mix_g=row(post_mix_g),
            pre_ffn_g=row(pre_ffn_g), w_ff1=w_ff1[l].astype(BF16), w_ff2=w_ff2[l].astype(BF16),
            post_ffn_g=row(post_ffn_g)))
    outs = []
    for x in (x_prompt, x_sample):
        batch, seq, _ = x.shape
        tabs = _rope_tables(seq)
        y = x.reshape(batch * seq, D_MODEL)
        for p in layers:
            y = _layer(y, batch, seq, tabs, ones_bd, p)
        outs.append(y.reshape(batch, seq, D_MODEL))
    return tuple(outs)
```

```python
import functools
import math

import jax
import jax.numpy as jnp
import numpy as np
from jax import lax
from jax.experimental import pallas as pl
from jax.experimental.pallas import tpu as pltpu

F32 = jnp.float32
BF16 = jnp.bfloat16

D_MODEL = 2048
GROUP_W = 512
HEAD_DIM = 64
N_HEADS = 8
N_KV = 2
N_REP = 4
KV_W = N_KV * HEAD_DIM
SCONV_K = 3
CONF_K = 31
WINDOW = 128
GRID_W = 64
ROPE_THETA = 10000.0
D_FF = 4 * D_MODEL
IN_W = 4096
RMS_EPS = 1e-6
LN_EPS = 1e-5
NEG_BIG = -1e30
LOG2_E = math.log2(math.e)

LANES = 128
BF16_SUBLANES = 16
F32_SUBLANES = 8
VMEM_LIMIT_BYTES = 56 * 1024 * 1024

COL_AB, COL_AC, COL_AV, COL_DA, COL_DG, COL_QB, COL_QC = (0, 512, 1024, 1536, 2048, 2560, 3072)
COL_KB, COL_VB, COL_KC, COL_VC = (3584, 3712, 3840, 3968)
_ORIG = dict(ab=0, ac=512, av=1024, qb=1536, kb=2048, vb=2176, qc=2304, kc=2816, vc=2944, da=3072, dg=3584)

IN_TN = 1024
IN_RC = 256
ATT_TQ = 256
ATT_TK = 512
ATT_M = N_REP * ATT_TQ
ATT_CW = 256
WIN_TK = ATT_TQ + 2 * WINDOW
V_ROWS = HEAD_DIM + BF16_SUBLANES
MIX_TM = 512
MIX_CH = 64
HALO = 16
OUT_TM = 512
OUT_RC = 256
FFN_TM = 512
FFN_TF = 1024
FFN_RC = 256


def _cparams(sem):
    return pltpu.CompilerParams(dimension_semantics=sem, vmem_limit_bytes=VMEM_LIMIT_BYTES)


def _row_chunks(rows, chunk):
    return [slice(r * chunk, (r + 1) * chunk) for r in range(rows // chunk)]


def _pipelined(chunks, produce, consume):
    nxt = produce(chunks[0])
    for n, rows in enumerate(chunks):
        cur = nxt
        if n + 1 < len(chunks):
            nxt = produce(chunks[n + 1])
        consume(rows, cur)


def _rope(x, cos, sin_signed, half, use_fwd):
    partner = jnp.where(use_fwd, pltpu.roll(x, half, 1), pltpu.roll(x, LANES - half, 1))
    return x * cos + partner * sin_signed


def _head_rms(x, gain, ones_bd):
    sq = x * x
    hi = sq.astype(BF16)
    lo = (sq - hi.astype(F32)).astype(BF16)
    ssum = (jnp.dot(hi, ones_bd, preferred_element_type=F32)
            + jnp.dot(lo, ones_bd, preferred_element_type=F32))
    return (x * lax.rsqrt(ssum * (1.0 / HEAD_DIM) + RMS_EPS)) * gain


def _in_proj_kernel(x_ref, g_ref, w_ref, cax_ref, sax_ref, cli_ref, sli_ref, qg_ref, kg_ref,
                    ones_ref, z_ref, h_ref):
    j = pl.program_id(1)
    chunks = _row_chunks(x_ref.shape[0], IN_RC)
    lane = lax.broadcasted_iota(jnp.int32, (1, LANES), 1)
    fwd_ax = (lane % 32) >= 16
    fwd_li = (lane % 64) >= 32

    def norm(rows):
        x = x_ref[rows, :]
        ms = jnp.mean(x * x, axis=-1, keepdims=True)
        h = ((x * lax.rsqrt(ms + RMS_EPS)) * g_ref[...]).astype(BF16)
        h_ref[rows, :] = h
        return h

    def mm(h):
        return jnp.dot(h, w_ref[...], preferred_element_type=F32)

    def store_plain(rows, h):
        z_ref[rows, :] = mm(h).astype(BF16)

    def store_dg_qb(rows, acc):
        z_ref[rows, 0:512] = acc[:, 0:512].astype(BF16)
        cax, sax = cax_ref[rows, :], sax_ref[rows, :]
        for c in range(4):
            lo = 512 + c * LANES
            x = _head_rms(acc[:, lo:lo + LANES], qg_ref[...], ones_ref[...])
            x = _rope(x, cax, sax, 16, fwd_ax) * (LOG2_E / math.sqrt(HEAD_DIM))
            z_ref[rows, lo:lo + LANES] = x.astype(BF16)

    def store_qc_kv(rows, acc):
        cax, sax = cax_ref[rows, :], sax_ref[rows, :]
        cli, sli = cli_ref[rows, :], sli_ref[rows, :]
        for c in range(4):
            lo = c * LANES
            x = _rope(acc[:, lo:lo + LANES], cli, sli, 32, fwd_li) * (LOG2_E / math.sqrt(HEAD_DIM))
            z_ref[rows, lo:lo + LANES] = x.astype(BF16)
        kb = _rope(_head_rms(acc[:, 512:640], kg_ref[...], ones_ref[...]), cax, sax, 16, fwd_ax)
        z_ref[rows, 512:640] = kb.astype(BF16)
        z_ref[rows, 640:768] = acc[:, 640:768].astype(BF16)
        z_ref[rows, 768:896] = _rope(acc[:, 768:896], cli, sli, 32, fwd_li).astype(BF16)
        z_ref[rows, 896:1024] = acc[:, 896:1024].astype(BF16)

    @pl.when(j == 0)
    def _():
        _pipelined(chunks, norm, store_plain)

    @pl.when(j == 1)
    def _():
        z_ref[...] = mm(h_ref[...]).astype(BF16)

    @pl.when(j == 2)
    def _():
        _pipelined(chunks, lambda rows: mm(h_ref[rows, :]), store_dg_qb)

    @pl.when(j == 3)
    def _():
        _pipelined(chunks, lambda rows: mm(h_ref[rows, :]), store_qc_kv)


def _in_proj(x2d, seq, g, w_bf, tabs, qg, kg, ones_bd):
    t = x2d.shape[0]
    tm = min(1024, seq)
    nseq = seq // tm
    row_spec = lambda width: pl.BlockSpec((tm, width), lambda i, j: (i % nseq, 0))
    const = lambda shape: pl.BlockSpec(shape, lambda i, j: (0, 0))
    return pl.pallas_call(
        _in_proj_kernel,
        grid=(t // tm, IN_W // IN_TN),
        in_specs=[
            pl.BlockSpec((tm, D_MODEL), lambda i, j: (i, 0)),
            const((1, D_MODEL)),
            pl.BlockSpec((D_MODEL, IN_TN), lambda i, j: (0, j)),
            row_spec(LANES), row_spec(LANES), row_spec(LANES), row_spec(LANES),
            const((1, LANES)), const((1, LANES)), const((LANES, LANES)),
        ],
        out_specs=pl.BlockSpec((tm, IN_TN), lambda i, j: (i, j)),
        out_shape=jax.ShapeDtypeStruct((t, IN_W), BF16),
        scratch_shapes=[pltpu.VMEM((tm, D_MODEL), BF16)],
        compiler_params=_cparams(("arbitrary", "arbitrary")),
        name="in_proj",
    )(x2d, g, w_bf, tabs[0], tabs[1], tabs[2], tabs[3], qg, kg, ones_bd)


def _build_qpad(q_ref, qpad_ref):
    qt = q_ref[...].astype(F32).T
    zeros = jnp.zeros((HEAD_DIM, ATT_M), BF16)
    for kvh in range(N_KV):
        qpad_ref[kvh, (1 - kvh) * HEAD_DIM:(2 - kvh) * HEAD_DIM, :] = zeros
        for g in range(N_REP):
            h = N_REP * kvh + g
            qpad_ref[kvh, kvh * HEAD_DIM:(kvh + 1) * HEAD_DIM, g * ATT_TQ:(g + 1) * ATT_TQ] = (
                qt[h * HEAD_DIM:(h + 1) * HEAD_DIM, :].astype(BF16))


def _store_heads(o_ref, ot_ref, o_by_kvh):
    for kvh in range(N_KV):
        for g in range(N_REP):
            h = N_REP * kvh + g
            ot_ref[h * HEAD_DIM:(h + 1) * HEAD_DIM, :] = o_by_kvh[kvh][:, g * ATT_TQ:(g + 1) * ATT_TQ]
    o_ref[...] = ot_ref[...].T.astype(BF16)


def _global_attn_kernel(q_ref, k_ref, v_ref, o_ref, qpad_ref, vt_ref, m_ref, acc_ref, ot_ref, s_ref, bm_ref,
                        *, nkb):
    i = pl.program_id(1)

    @pl.when(i == 0)
    def _():
        ones = jnp.ones((BF16_SUBLANES, ATT_TK), BF16)

        def fill(c, carry):
            r0 = pl.multiple_of(c * ATT_TK, ATT_TK)
            vt = v_ref[pl.ds(r0, ATT_TK), :].astype(F32).T
            for kvh in range(N_KV):
                vt_ref[kvh, c, 0:HEAD_DIM, :] = vt[kvh * HEAD_DIM:(kvh + 1) * HEAD_DIM, :].astype(BF16)
                vt_ref[kvh, c, HEAD_DIM:V_ROWS, :] = ones
            return carry

        lax.fori_loop(0, nkb, fill, 0)

    _build_qpad(q_ref, qpad_ref)
    m_ref[...] = jnp.full(m_ref.shape, NEG_BIG, F32)
    acc_ref[...] = jnp.zeros(acc_ref.shape, F32)

    chunks = [(kvh, slice(j * ATT_CW, (j + 1) * ATT_CW))
              for kvh in range(N_KV) for j in range(ATT_M // ATT_CW)]

    def load_keys(c):
        r0 = pl.multiple_of(c * ATT_TK, ATT_TK)
        return k_ref[pl.ds(r0, ATT_TK), :]

    def scores(kb, slot, kvh, cols):
        s = jnp.dot(kb, qpad_ref[kvh, :, cols], preferred_element_type=F32)
        s_ref[slot, kvh, :, cols] = s
        bm_ref[slot, kvh, :, cols] = jnp.max(s, axis=0, keepdims=True)

    def softmax_pv(c, slot, kvh, cols):
        m_old = m_ref[kvh, :, cols]
        m_new = jnp.maximum(m_old, bm_ref[slot, kvh, :, cols])
        alpha = jnp.exp2(m_old - m_new)
        p = jnp.exp2(s_ref[slot, kvh, :, cols] - m_new).astype(BF16)
        pv = jnp.dot(vt_ref[kvh, c], p, preferred_element_type=F32)
        acc_ref[kvh, :, cols] = acc_ref[kvh, :, cols] * alpha + pv
        m_ref[kvh, :, cols] = m_new

    kb0 = load_keys(0)
    for kvh, cols in chunks:
        scores(kb0, 0, kvh, cols)

    def body(it, carry):
        c = 2 * it
        kb = load_keys(c + 1)
        for kvh, cols in chunks:
            scores(kb, 1, kvh, cols)
            softmax_pv(c, 0, kvh, cols)
        kb = load_keys(jnp.minimum(c + 2, nkb - 1))
        for kvh, cols in chunks:
            scores(kb, 0, kvh, cols)
            softmax_pv(c + 1, 1, kvh, cols)
        return carry

    lax.fori_loop(0, nkb // 2, body, 0)

    outs = []
    for kvh in range(N_KV):
        acc = acc_ref[kvh]
        outs.append(acc[0:HEAD_DIM, :] / acc[HEAD_DIM:HEAD_DIM + 1, :])
    _store_heads(o_ref, ot_ref, outs)


def _global_attn(z, batch, seq):
    nq = seq // ATT_TQ
    nkb = seq // ATT_TK
    return pl.pallas_call(
        functools.partial(_global_attn_kernel, nkb=nkb),
        grid=(batch, nq),
        in_specs=[
            pl.BlockSpec((ATT_TQ, GROUP_W), lambda b, i: (b * nq + i, COL_QB // GROUP_W)),
            pl.BlockSpec((seq, KV_W), lambda b, i: (b, COL_KB // KV_W)),
            pl.BlockSpec((seq, KV_W), lambda b, i: (b, COL_VB // KV_W)),
        ],
        out_specs=pl.BlockSpec((ATT_TQ, GROUP_W), lambda b, i: (b * nq + i, 0)),
        out_shape=jax.ShapeDtypeStruct((batch * seq, GROUP_W), BF16),
        scratch_shapes=[
            pltpu.VMEM((N_KV, KV_W, ATT_M), BF16),
            pltpu.VMEM((N_KV, nkb, V_ROWS, ATT_TK), BF16),
            pltpu.VMEM((N_KV, 1, ATT_M), F32),
            pltpu.VMEM((N_KV, V_ROWS, ATT_M), F32),
            pltpu.VMEM((GROUP_W, ATT_TQ), F32),
            pltpu.VMEM((2, N_KV, ATT_TK, ATT_M), F32),
            pltpu.VMEM((2, N_KV, 1, ATT_M), F32),
        ],
        compiler_params=_cparams(("arbitrary", "arbitrary")),
        name="global_attn",
    )(z, z, z)


def _window_attn_kernel(q_ref, k_ref, v_ref, sink_ref, o_ref, qpad_ref, ot_ref, *, seq):
    i = pl.program_id(1)
    _build_qpad(q_ref, qpad_ref)
    q0 = i * ATT_TQ
    start = pl.multiple_of(jnp.clip(q0 - WINDOW, 0, seq - WIN_TK), WINDOW)
    kb = k_ref[pl.ds(start, WIN_TK), :]
    vt = v_ref[pl.ds(start, WIN_TK), :].astype(F32).T
    kpos = start + lax.broadcasted_iota(jnp.int32, (WIN_TK, ATT_TQ), 0)
    qpos = q0 + lax.broadcasted_iota(jnp.int32, (WIN_TK, ATT_TQ), 1)
    valid = jnp.abs(qpos - kpos) <= WINDOW
    ones = jnp.ones((BF16_SUBLANES, WIN_TK), BF16)
    vks = [jnp.concatenate([vt[kvh * HEAD_DIM:(kvh + 1) * HEAD_DIM, :].astype(BF16), ones], axis=0)
           for kvh in range(N_KV)]
    heads = [(kvh, g) for kvh in range(N_KV) for g in range(N_REP)]

    def scores(kvh, g):
        s = jnp.dot(kb, qpad_ref[kvh, :, g * ATT_TQ:(g + 1) * ATT_TQ], preferred_element_type=F32)
        return jnp.where(valid, s, NEG_BIG)

    def finish(kvh, g, s):
        sk = sink_ref[kvh, :, g * ATT_TQ:(g + 1) * ATT_TQ]
        m = jnp.maximum(jnp.max(s, axis=0, keepdims=True), sk)
        p = jnp.exp2(s - m).astype(BF16)
        o = jnp.dot(vks[kvh], p, preferred_element_type=F32)
        denom = o[HEAD_DIM:HEAD_DIM + 1, :] + jnp.exp2(sk - m)
        h = N_REP * kvh + g
        ot_ref[h * HEAD_DIM:(h + 1) * HEAD_DIM, :] = o[0:HEAD_DIM, :] / denom

    s_next = scores(*heads[0])
    for n, head in enumerate(heads):
        s_cur = s_next
        if n + 1 < len(heads):
            s_next = scores(*heads[n + 1])
        finish(*head, s_cur)
    o_ref[...] = ot_ref[...].T.astype(BF16)


def _window_attn(z, sink_rows, batch, seq):
    nq = seq // ATT_TQ
    return pl.pallas_call(
        functools.partial(_window_attn_kernel, seq=seq),
        grid=(batch, nq),
        in_specs=[
            pl.BlockSpec((ATT_TQ, GROUP_W), lambda b, i: (b * nq + i, COL_QC // GROUP_W)),
            pl.BlockSpec((seq, KV_W), lambda b, i: (b, COL_KC // KV_W)),
            pl.BlockSpec((seq, KV_W), lambda b, i: (b, COL_VC // KV_W)),
            pl.BlockSpec((N_KV, 1, ATT_M), lambda b, i: (0, 0, 0)),
        ],
        out_specs=pl.BlockSpec((ATT_TQ, GROUP_W), lambda b, i: (b * nq + i, 0)),
        out_shape=jax.ShapeDtypeStruct((batch * seq, GROUP_W), BF16),
        scratch_shapes=[
            pltpu.VMEM((N_KV, KV_W, ATT_M), BF16),
            pltpu.VMEM((GROUP_W, ATT_TQ), F32),
        ],
        compiler_params=_cparams(("arbitrary", "arbitrary")),
        name="window_attn",
    )(z, z, z, sink_rows)


def _mixer_kernel(ab_ref, ac_ref, acp_ref, acn_ref, av_ref, avp_ref, avn_ref,
                  da_ref, dap_ref, dan_ref, dg_ref, dgp_ref, dgn_ref,
                  wa_ref, wd_ref, bd_ref, lng_ref, lnb_ref,
                  ya_ref, yd_ref, pbuf, ubuf, ush, *, nseq):
    i = pl.program_id(0)
    keep_prev = jnp.where(i % nseq == 0, 0.0, 1.0)
    keep_next = jnp.where(i % nseq == nseq - 1, 0.0, 1.0)

    def gate(a_ref, g_ref):
        return a_ref[...].astype(F32) * jax.nn.sigmoid(g_ref[...].astype(F32))

    def prod(a_ref, b_ref):
        return a_ref[...].astype(F32) * b_ref[...].astype(F32)

    tm = MIX_TM
    ubuf[0:HALO, :] = gate(dap_ref, dgp_ref) * keep_prev
    ubuf[HALO:HALO + tm, :] = gate(da_ref, dg_ref)
    ubuf[HALO + tm:2 * HALO + tm, :] = gate(dan_ref, dgn_ref) * keep_next
    pbuf[0:HALO, :] = prod(acp_ref, avp_ref) * keep_prev
    pbuf[HALO:HALO + tm, :] = prod(ac_ref, av_ref)
    pbuf[HALO + tm:2 * HALO + tm, :] = prod(acn_ref, avn_ref) * keep_next

    rows = tm + 2 * HALO
    ubuf[rows:rows + F32_SUBLANES, :] = jnp.zeros((F32_SUBLANES, GROUP_W), F32)
    for s in range(1, F32_SUBLANES):
        ush[s - 1, :, :] = ubuf[s:s + rows, :]

    bd, lng, lnb = bd_ref[...], lng_ref[...], lnb_ref[...]

    for c in range(tm // MIX_CH):
        r0 = c * MIX_CH
        acc = jnp.zeros((MIX_CH, GROUP_W), F32)
        for k in range(CONF_K):
            lo = r0 + HALO - CONF_K // 2 + k
            s = lo % F32_SUBLANES
            a = lo - s
            tap = ubuf[a:a + MIX_CH, :] if s == 0 else ush[s - 1, a:a + MIX_CH, :]
            wk = wd_ref[k * F32_SUBLANES:(k + 1) * F32_SUBLANES, :]
            acc = acc + (tap.reshape(MIX_CH // F32_SUBLANES, F32_SUBLANES, GROUP_W) * wk[None]
                         ).reshape(MIX_CH, GROUP_W)
        acc = acc + bd
        mu = jnp.mean(acc, axis=-1, keepdims=True)
        xc = acc - mu
        var = jnp.mean(xc * xc, axis=-1, keepdims=True)
        y = (xc * lax.rsqrt(var + LN_EPS)) * lng + lnb
        yd_ref[r0:r0 + MIX_CH, :] = (y * jax.nn.sigmoid(y)).astype(BF16)
        pa = jnp.zeros((MIX_CH, GROUP_W), F32)
        for k in range(SCONV_K):
            lo = r0 + HALO - SCONV_K // 2 + k
            pa = pa + wa_ref[k:k + 1, :] * pbuf[lo:lo + MIX_CH, :]
        ya_ref[r0:r0 + MIX_CH, :] = (ab_ref[r0:r0 + MIX_CH, :].astype(F32) * pa).astype(BF16)


def _mixer(z, seq, wa, wd, bd, lng, lnb):
    t = z.shape[0]
    tm = MIX_TM
    nseq = seq // tm
    hb = tm // HALO
    nh = t // HALO

    def cur(col):
        return pl.BlockSpec((tm, GROUP_W), lambda i: (i, col // GROUP_W))

    def prev(col):
        return pl.BlockSpec((HALO, GROUP_W), lambda i: (jnp.maximum(i * hb - 1, 0), col // GROUP_W))

    def nxt(col):
        return pl.BlockSpec((HALO, GROUP_W), lambda i: (jnp.minimum((i + 1) * hb, nh - 1), col // GROUP_W))

    const = lambda shape: pl.BlockSpec(shape, lambda i: (0, 0))
    halo_cols = (COL_AC, COL_AV, COL_DA, COL_DG)
    in_specs = [cur(COL_AB)]
    for col in halo_cols:
        in_specs += [cur(col), prev(col), nxt(col)]
    in_specs += [const((SCONV_K, GROUP_W)), const((CONF_K * F32_SUBLANES, GROUP_W)), const((1, GROUP_W)),
                 const((1, GROUP_W)), const((1, GROUP_W))]
    out_spec = pl.BlockSpec((tm, GROUP_W), lambda i: (i, 0))
    return pl.pallas_call(
        functools.partial(_mixer_kernel, nseq=nseq),
        grid=(t // tm,),
        in_specs=in_specs,
        out_specs=[out_spec, out_spec],
        out_shape=[jax.ShapeDtypeStruct((t, GROUP_W), BF16)] * 2,
        scratch_shapes=[pltpu.VMEM((tm + 2 * HALO, GROUP_W), F32),
                        pltpu.VMEM((tm + 2 * HALO + F32_SUBLANES, GROUP_W), F32),
                        pltpu.VMEM((F32_SUBLANES - 1, tm + 2 * HALO, GROUP_W), F32)],
        compiler_params=_cparams(("arbitrary",)),
        name="mixer",
    )(*([z] * 13), wa, wd, bd, lng, lnb)


def _out_proj_kernel(ya_ref, yb_ref, yc_ref, yd_ref, w_ref, x_ref, g_ref, o_ref):
    def project(rows):
        y = jnp.concatenate([ya_ref[rows, :], yb_ref[rows, :], yc_ref[rows, :], yd_ref[rows, :]], axis=-1)
        return jnp.dot(y, w_ref[...], preferred_element_type=F32)

    def residual(rows, acc):
        ms = jnp.mean(acc * acc, axis=-1, keepdims=True)
        o_ref[rows, :] = x_ref[rows, :] + (acc * lax.rsqrt(ms + RMS_EPS)) * g_ref[...]

    _pipelined(_row_chunks(x_ref.shape[0], OUT_RC), project, residual)


def _out_proj(ya, yb, yc, yd, w_bf, x2d, g):
    t = x2d.shape[0]
    tm = OUT_TM
    yspec = pl.BlockSpec((tm, GROUP_W), lambda i: (i, 0))
    xspec = pl.BlockSpec((tm, D_MODEL), lambda i: (i, 0))
    return pl.pallas_call(
        _out_proj_kernel,
        grid=(t // tm,),
        in_specs=[yspec, yspec, yspec, yspec,
                  pl.BlockSpec((D_MODEL, D_MODEL), lambda i: (0, 0)),
                  xspec,
                  pl.BlockSpec((1, D_MODEL), lambda i: (0, 0))],
        out_specs=xspec,
        out_shape=jax.ShapeDtypeStruct((t, D_MODEL), F32),
        compiler_params=_cparams(("arbitrary",)),
        name="out_proj",
    )(ya, yb, yc, yd, w_bf, x2d, g)


def _ffn_kernel(x_ref, g1_ref, w1_ref, w2_ref, g2_ref, o_ref, h_ref, acc_ref, *, nf):
    j = pl.program_id(1)
    chunks = _row_chunks(x_ref.shape[0], FFN_RC)

    def norm(rows):
        x = x_ref[rows, :]
        ms = jnp.mean(x * x, axis=-1, keepdims=True)
        h = ((x * lax.rsqrt(ms + RMS_EPS)) * g1_ref[...]).astype(BF16)
        h_ref[rows, :] = h
        return h

    def mlp(h):
        u = jnp.maximum(jnp.dot(h, w1_ref[...], preferred_element_type=F32), 0.0)
        return jnp.dot((u * u).astype(BF16), w2_ref[...], preferred_element_type=F32)

    def first(rows, h):
        acc_ref[rows, :] = mlp(h)

    def last(rows, y):
        ms = jnp.mean(y * y, axis=-1, keepdims=True)
        o_ref[rows, :] = x_ref[rows, :] + (y * lax.rsqrt(ms + RMS_EPS)) * g2_ref[...]

    @pl.when(j == 0)
    def _():
        _pipelined(chunks, norm, first)

    @pl.when(jnp.logical_and(j > 0, j < nf - 1))
    def _():
        acc_ref[...] += mlp(h_ref[...])

    @pl.when(j == nf - 1)
    def _():
        _pipelined(chunks, lambda rows: acc_ref[rows, :] + mlp(h_ref[rows, :]), last)


def _ffn(x2d, g1, w1_bf, w2_bf, g2):
    t = x2d.shape[0]
    tm, tf = FFN_TM, FFN_TF
    nf = D_FF // tf
    xspec = pl.BlockSpec((tm, D_MODEL), lambda i, j: (i, 0))
    gspec = pl.BlockSpec((1, D_MODEL), lambda i, j: (0, 0))
    return pl.pallas_call(
        functools.partial(_ffn_kernel, nf=nf),
        grid=(t // tm, nf),
        in_specs=[xspec, gspec,
                  pl.BlockSpec((D_MODEL, tf), lambda i, j: (0, j)),
                  pl.BlockSpec((tf, D_MODEL), lambda i, j: (j, 0)),
                  gspec],
        out_specs=xspec,
        out_shape=jax.ShapeDtypeStruct((t, D_MODEL), F32),
        scratch_shapes=[pltpu.VMEM((tm, D_MODEL), BF16), pltpu.VMEM((tm, D_MODEL), F32)],
        compiler_params=_cparams(("arbitrary", "arbitrary")),
        name="ffn",
    )(x2d, g1, w1_bf, w2_bf, g2)


def _rope_tables(seq):
    lane = np.arange(LANES)
    t = jnp.arange(seq)
    inv_ax = 1.0 / (ROPE_THETA ** (jnp.arange(0, 32, 2, dtype=F32) / 32))
    pos_row = (t // GRID_W).astype(F32)
    pos_col = (t % GRID_W).astype(F32)
    use_col = ((lane % HEAD_DIM) >= 32)
    pos = jnp.where(use_col[None, :], pos_col[:, None], pos_row[:, None])
    ang = pos * inv_ax[lane % 16][None, :]
    sign_ax = np.where((lane % 32) >= 16, 1.0, -1.0).astype(np.float32)
    cax, sax = jnp.cos(ang), jnp.sin(ang) * sign_ax[None, :]
    inv_li = 1.0 / (ROPE_THETA ** (jnp.arange(0, HEAD_DIM, 2, dtype=F32) / HEAD_DIM))
    ang = t.astype(F32)[:, None] * inv_li[lane % 32][None, :]
    sign_li = np.where((lane % HEAD_DIM) >= 32, 1.0, -1.0).astype(np.float32)
    cli, sli = jnp.cos(ang), jnp.sin(ang) * sign_li[None, :]
    return cax, sax, cli, sli


def _permute_w_in(w):
    o = _ORIG
    order = [("ab", 512), ("ac", 512), ("av", 512), ("da", 512), ("dg", 512), ("qb", 512), ("qc", 512),
             ("kb", 128), ("vb", 128), ("kc", 128), ("vc", 128)]
    return jnp.concatenate([w[:, o[n]:o[n] + wd] for n, wd in order], axis=1).astype(BF16)


def _layer(x2d, batch, seq, tabs, ones_bd, p):
    z = _in_proj(x2d, seq, p["pre_mix_g"], p["w_in"], tabs, p["q_norm_g"], p["k_norm_g"], ones_bd)
    yb = _global_attn(z, batch, seq)
    yc = _window_attn(z, p["sink_rows"], batch, seq)
    ya, yd = _mixer(z, seq, p["conv_a_w"], p["conv_d_w"], p["conv_d_b"], p["ln_d_g"], p["ln_d_b"])
    x1 = _out_proj(ya, yb, yc, yd, p["w_out"], x2d, p["post_mix_g"])
    return _ffn(x1, p["pre_ffn_g"], p["w_ff1"], p["w_ff2"], p["post_ffn_g"])


def kernel(x_prompt, x_sample, pre_mix_g, w_in, conv_a_w, q_norm_g, k_norm_g, sink_c, conv_d_w, conv_d_b,
           ln_d_g, ln_d_b, w_out, post_mix_g, pre_ffn_g, w_ff1, w_ff2, post_ffn_g):
    depth = w_in.shape[0]
    lane = np.arange(LANES)
    ones_bd = jnp.asarray((lane[:, None] // HEAD_DIM) == (lane[None, :] // HEAD_DIM), BF16)
    layers = []
    for l in range(depth):
        row = lambda a: a[l].reshape(1, -1).astype(F32)
        sink_rows = jnp.repeat((sink_c[l].astype(F32) * LOG2_E).reshape(N_KV, 1, N_REP), ATT_TQ, axis=2)
        layers.append(dict(
            pre_mix_g=row(pre_mix_g), w_in=_permute_w_in(w_in[l]),
            conv_a_w=conv_a_w[l].astype(F32),
            conv_d_w=jnp.repeat(conv_d_w[l].astype(F32), F32_SUBLANES, axis=0),
            conv_d_b=row(conv_d_b), ln_d_g=row(ln_d_g), ln_d_b=row(ln_d_b),
            q_norm_g=jnp.tile(row(q_norm_g), (1, 2)), k_norm_g=jnp.tile(row(k_norm_g), (1, 2)),
            sink_rows=sink_rows, w_out=w_out[l].astype(BF16), post_mix_g=row(post_mix_g),
            pre_ffn_g=row(pre_ffn_g), w_ff1=w_ff1[l].astype(BF16), w_ff2=w_ff2[l].astype(BF16),
            post_ffn_g=row(post_ffn_g)))
    outs = []
    for x in (x_prompt, x_sample):
        batch, seq, _ = x.shape
        tabs = _rope_tables(seq)
        y = x.reshape(batch * seq, D_MODEL)
        for p in layers:
            y = _layer(y, batch, seq, tabs, ones_bd, p)
        outs.append(y.reshape(batch, seq, D_MODEL))
    return tuple(outs)
```

```python
import functools
import math

import jax
import jax.numpy as jnp
import numpy as np
from jax import lax
from jax.experimental import pallas as pl
from jax.experimental.pallas import tpu as pltpu

F32 = jnp.float32
BF16 = jnp.bfloat16

D_MODEL = 2048
GROUP_W = 512
HEAD_DIM = 64
N_HEADS = 8
N_KV = 2
N_REP = 4
KV_W = N_KV * HEAD_DIM
SCONV_K = 3
CONF_K = 31
WINDOW = 128
GRID_W = 64
ROPE_THETA = 10000.0
D_FF = 4 * D_MODEL
IN_W = 4096
RMS_EPS = 1e-6
LN_EPS = 1e-5
NEG_BIG = -1e30
LOG2_E = math.log2(math.e)

LANES = 128
BF16_SUBLANES = 16
F32_SUBLANES = 8
VMEM_LIMIT_BYTES = 56 * 1024 * 1024

COL_AB, COL_AC, COL_AV, COL_DA, COL_DG, COL_QB, COL_QC = (0, 512, 1024, 1536, 2048, 2560, 3072)
COL_KB, COL_VB, COL_KC, COL_VC = (3584, 3712, 3840, 3968)
_ORIG = dict(ab=0, ac=512, av=1024, qb=1536, kb=2048, vb=2176, qc=2304, kc=2816, vc=2944, da=3072, dg=3584)

IN_TN = 1024
IN_RC = 256
ATT_TQ = 256
ATT_TK = 512
ATT_M = N_REP * ATT_TQ
ATT_CW = 256
ATT_UNROLL = 8
WIN_TK = ATT_TQ + 2 * WINDOW
V_ROWS = HEAD_DIM + BF16_SUBLANES
MIX_TM = 512
MIX_CH = 64
HALO = 16
OUT_RC = 256
FFN_TM = 512
FFN_TF = 1024
FFN_RC = 256


def _cparams(sem):
    return pltpu.CompilerParams(dimension_semantics=sem, vmem_limit_bytes=VMEM_LIMIT_BYTES)


def _row_chunks(rows, chunk):
    return [slice(r * chunk, (r + 1) * chunk) for r in range(rows // chunk)]


def _pipelined(chunks, produce, consume):
    nxt = produce(chunks[0])
    for n, rows in enumerate(chunks):
        cur = nxt
        if n + 1 < len(chunks):
            nxt = produce(chunks[n + 1])
        consume(rows, cur)


def _rope(x, cos, sin_signed, half, use_fwd):
    partner = jnp.where(use_fwd, pltpu.roll(x, half, 1), pltpu.roll(x, LANES - half, 1))
    return x * cos + partner * sin_signed


def _head_rms(x, gain, ones_bd):
    sq = x * x
    hi = sq.astype(BF16)
    lo = (sq - hi.astype(F32)).astype(BF16)
    ssum = (jnp.dot(hi, ones_bd, preferred_element_type=F32)
            + jnp.dot(lo, ones_bd, preferred_element_type=F32))
    return (x * lax.rsqrt(ssum * (1.0 / HEAD_DIM) + RMS_EPS)) * gain


def _in_proj_kernel(x_ref, g_ref, w_ref, cax_ref, sax_ref, cli_ref, sli_ref, qg_ref, kg_ref,
                    ones_ref, z_ref, h_ref):
    j = pl.program_id(1)
    chunks = _row_chunks(x_ref.shape[0], IN_RC)
    lane = lax.broadcasted_iota(jnp.int32, (1, LANES), 1)
    fwd_ax = (lane % 32) >= 16
    fwd_li = (lane % 64) >= 32

    def norm(rows):
        x = x_ref[rows, :]
        ms = jnp.mean(x * x, axis=-1, keepdims=True)
        h = ((x * lax.rsqrt(ms + RMS_EPS)) * g_ref[...]).astype(BF16)
        h_ref[rows, :] = h
        return h

    def mm(h):
        return jnp.dot(h, w_ref[...], preferred_element_type=F32)

    def store_plain(rows, h):
        z_ref[rows, :] = mm(h).astype(BF16)

    def store_dg_qb(rows, acc):
        z_ref[rows, 0:512] = acc[:, 0:512].astype(BF16)
        cax, sax = cax_ref[rows, :], sax_ref[rows, :]
        for c in range(4):
            lo = 512 + c * LANES
            x = _head_rms(acc[:, lo:lo + LANES], qg_ref[...], ones_ref[...])
            x = _rope(x, cax, sax, 16, fwd_ax) * (LOG2_E / math.sqrt(HEAD_DIM))
            z_ref[rows, lo:lo + LANES] = x.astype(BF16)

    def store_qc_kv(rows, acc):
        cax, sax = cax_ref[rows, :], sax_ref[rows, :]
        cli, sli = cli_ref[rows, :], sli_ref[rows, :]
        for c in range(4):
            lo = c * LANES
            x = _rope(acc[:, lo:lo + LANES], cli, sli, 32, fwd_li) * (LOG2_E / math.sqrt(HEAD_DIM))
            z_ref[rows, lo:lo + LANES] = x.astype(BF16)
        kb = _rope(_head_rms(acc[:, 512:640], kg_ref[...], ones_ref[...]), cax, sax, 16, fwd_ax)
        z_ref[rows, 512:640] = kb.astype(BF16)
        z_ref[rows, 640:768] = acc[:, 640:768].astype(BF16)
        z_ref[rows, 768:896] = _rope(acc[:, 768:896], cli, sli, 32, fwd_li).astype(BF16)
        z_ref[rows, 896:1024] = acc[:, 896:1024].astype(BF16)

    @pl.when(j == 0)
    def _():
        _pipelined(chunks, norm, store_plain)

    @pl.when(j == 1)
    def _():
        z_ref[...] = mm(h_ref[...]).astype(BF16)

    @pl.when(j == 2)
    def _():
        _pipelined(chunks, lambda rows: mm(h_ref[rows, :]), store_dg_qb)

    @pl.when(j == 3)
    def _():
        _pipelined(chunks, lambda rows: mm(h_ref[rows, :]), store_qc_kv)


def _in_proj(x2d, seq, g, w_bf, tabs, qg, kg, ones_bd):
    t = x2d.shape[0]
    tm = min(1024, seq)
    nseq = seq // tm
    row_spec = lambda width: pl.BlockSpec((tm, width), lambda i, j: (i % nseq, 0))
    const = lambda shape: pl.BlockSpec(shape, lambda i, j: (0, 0))
    return pl.pallas_call(
        _in_proj_kernel,
        grid=(t // tm, IN_W // IN_TN),
        in_specs=[
            pl.BlockSpec((tm, D_MODEL), lambda i, j: (i, 0)),
            const((1, D_MODEL)),
            pl.BlockSpec((D_MODEL, IN_TN), lambda i, j: (0, j)),
            row_spec(LANES), row_spec(LANES), row_spec(LANES), row_spec(LANES),
            const((1, LANES)), const((1, LANES)), const((LANES, LANES)),
        ],
        out_specs=pl.BlockSpec((tm, IN_TN), lambda i, j: (i, j)),
        out_shape=jax.ShapeDtypeStruct((t, IN_W), BF16),
        scratch_shapes=[pltpu.VMEM((tm, D_MODEL), BF16)],
        compiler_params=_cparams(("arbitrary", "arbitrary")),
        name="in_proj",
    )(x2d, g, w_bf, tabs[0], tabs[1], tabs[2], tabs[3], qg, kg, ones_bd)


def _build_qpad(q_ref, qpad_ref):
    qt = q_ref[...].astype(F32).T
    zeros = jnp.zeros((HEAD_DIM, ATT_M), BF16)
    for kvh in range(N_KV):
        qpad_ref[kvh, (1 - kvh) * HEAD_DIM:(2 - kvh) * HEAD_DIM, :] = zeros
        for g in range(N_REP):
            h = N_REP * kvh + g
            qpad_ref[kvh, kvh * HEAD_DIM:(kvh + 1) * HEAD_DIM, g * ATT_TQ:(g + 1) * ATT_TQ] = (
                qt[h * HEAD_DIM:(h + 1) * HEAD_DIM, :].astype(BF16))


def _store_heads(o_ref, ot_ref, o_by_kvh):
    for kvh in range(N_KV):
        for g in range(N_REP):
            h = N_REP * kvh + g
            ot_ref[h * HEAD_DIM:(h + 1) * HEAD_DIM, :] = o_by_kvh[kvh][:, g * ATT_TQ:(g + 1) * ATT_TQ]
    o_ref[...] = ot_ref[...].T.astype(BF16)


def _global_attn_kernel(q_ref, k_ref, v_ref, o_ref, qpad_ref, vt_ref, m_ref, acc_ref, ot_ref, s_ref, bm_ref,
                        *, nkb):
    i = pl.program_id(1)

    @pl.when(i == 0)
    def _():
        ones = jnp.ones((BF16_SUBLANES, ATT_TK), BF16)

        def fill(c, carry):
            r0 = pl.multiple_of(c * ATT_TK, ATT_TK)
            vt = v_ref[pl.ds(r0, ATT_TK), :].astype(F32).T
            for kvh in range(N_KV):
                vt_ref[kvh, c, 0:HEAD_DIM, :] = vt[kvh * HEAD_DIM:(kvh + 1) * HEAD_DIM, :].astype(BF16)
                vt_ref[kvh, c, HEAD_DIM:V_ROWS, :] = ones
            return carry

        lax.fori_loop(0, nkb, fill, 0)

    _build_qpad(q_ref, qpad_ref)
    m_ref[...] = jnp.full(m_ref.shape, NEG_BIG, F32)
    acc_ref[...] = jnp.zeros(acc_ref.shape, F32)

    chunks = [(kvh, slice(j * ATT_CW, (j + 1) * ATT_CW))
              for kvh in range(N_KV) for j in range(ATT_M // ATT_CW)]

    def load_keys(c):
        r0 = pl.multiple_of(c * ATT_TK, ATT_TK)
        return k_ref[pl.ds(r0, ATT_TK), :]

    def scores(kb, slot, kvh, cols):
        s = jnp.dot(kb, qpad_ref[kvh, :, cols], preferred_element_type=F32)
        s_ref[slot, kvh, :, cols] = s
        bm_ref[slot, kvh, :, cols] = jnp.max(s, axis=0, keepdims=True)

    def softmax_pv(c, slot, kvh, cols):
        m_old = m_ref[kvh, :, cols]
        m_new = jnp.maximum(m_old, bm_ref[slot, kvh, :, cols])
        alpha = jnp.exp2(m_old - m_new)
        p = jnp.exp2(s_ref[slot, kvh, :, cols] - m_new).astype(BF16)
        pv = jnp.dot(vt_ref[kvh, c], p, preferred_element_type=F32)
        acc_ref[kvh, :, cols] = acc_ref[kvh, :, cols] * alpha + pv
        m_ref[kvh, :, cols] = m_new

    kb0 = load_keys(0)
    for kvh, cols in chunks:
        scores(kb0, 0, kvh, cols)

    unroll = math.gcd(nkb, ATT_UNROLL)

    def body(it, carry):
        for u in range(unroll):
            c = unroll * it + u
            slot = u % 2
            kb = load_keys(jnp.minimum(c + 1, nkb - 1))
            for kvh, cols in chunks:
                scores(kb, 1 - slot, kvh, cols)
                softmax_pv(c, slot, kvh, cols)
        return carry

    lax.fori_loop(0, nkb // unroll, body, 0)

    outs = []
    for kvh in range(N_KV):
        acc = acc_ref[kvh]
        outs.append(acc[0:HEAD_DIM, :] / acc[HEAD_DIM:HEAD_DIM + 1, :])
    _store_heads(o_ref, ot_ref, outs)


def _global_attn(z, batch, seq):
    nq = seq // ATT_TQ
    nkb = seq // ATT_TK
    assert seq % (2 * ATT_TK) == 0, "the two score slots alternate: an even number of key blocks is required"
    return pl.pallas_call(
        functools.partial(_global_attn_kernel, nkb=nkb),
        grid=(batch, nq),
        in_specs=[
            pl.BlockSpec((ATT_TQ, GROUP_W), lambda b, i: (b * nq + i, COL_QB // GROUP_W)),
            pl.BlockSpec((seq, KV_W), lambda b, i: (b, COL_KB // KV_W)),
            pl.BlockSpec((seq, KV_W), lambda b, i: (b, COL_VB // KV_W)),
        ],
        out_specs=pl.BlockSpec((ATT_TQ, GROUP_W), lambda b, i: (b * nq + i, 0)),
        out_shape=jax.ShapeDtypeStruct((batch * seq, GROUP_W), BF16),
        scratch_shapes=[
            pltpu.VMEM((N_KV, KV_W, ATT_M), BF16),
            pltpu.VMEM((N_KV, nkb, V_ROWS, ATT_TK), BF16),
            pltpu.VMEM((N_KV, 1, ATT_M), F32),
            pltpu.VMEM((N_KV, V_ROWS, ATT_M), F32),
            pltpu.VMEM((GROUP_W, ATT_TQ), F32),
            pltpu.VMEM((2, N_KV, ATT_TK, ATT_M), F32),
            pltpu.VMEM((2, N_KV, 1, ATT_M), F32),
        ],
        compiler_params=_cparams(("arbitrary", "arbitrary")),
        name="global_attn",
    )(z, z, z)


def _window_attn_kernel(q_ref, k_ref, v_ref, sink_ref, o_ref, qpad_ref, ot_ref, *, seq):
    i = pl.program_id(1)
    _build_qpad(q_ref, qpad_ref)
    q0 = i * ATT_TQ
    start = pl.multiple_of(jnp.clip(q0 - WINDOW, 0, seq - WIN_TK), WINDOW)
    kb = k_ref[pl.ds(start, WIN_TK), :]
    vt = v_ref[pl.ds(start, WIN_TK), :].astype(F32).T
    kpos = start + lax.broadcasted_iota(jnp.int32, (WIN_TK, ATT_TQ), 0)
    qpos = q0 + lax.broadcasted_iota(jnp.int32, (WIN_TK, ATT_TQ), 1)
    valid = jnp.abs(qpos - kpos) <= WINDOW
    ones = jnp.ones((BF16_SUBLANES, WIN_TK), BF16)
    vks = [jnp.concatenate([vt[kvh * HEAD_DIM:(kvh + 1) * HEAD_DIM, :].astype(BF16), ones], axis=0)
           for kvh in range(N_KV)]
    heads = [(kvh, g) for kvh in range(N_KV) for g in range(N_REP)]

    def scores(kvh, g):
        s = jnp.dot(kb, qpad_ref[kvh, :, g * ATT_TQ:(g + 1) * ATT_TQ], preferred_element_type=F32)
        return jnp.where(valid, s, NEG_BIG)

    def finish(kvh, g, s):
        sk = sink_ref[kvh, :, g * ATT_TQ:(g + 1) * ATT_TQ]
        m = jnp.maximum(jnp.max(s, axis=0, keepdims=True), sk)
        p = jnp.exp2(s - m).astype(BF16)
        o = jnp.dot(vks[kvh], p, preferred_element_type=F32)
        denom = o[HEAD_DIM:HEAD_DIM + 1, :] + jnp.exp2(sk - m)
        h = N_REP * kvh + g
        ot_ref[h * HEAD_DIM:(h + 1) * HEAD_DIM, :] = o[0:HEAD_DIM, :] / denom

    s_next = scores(*heads[0])
    for n, head in enumerate(heads):
        s_cur = s_next
        if n + 1 < len(heads):
            s_next = scores(*heads[n + 1])
        finish(*head, s_cur)
    o_ref[...] = ot_ref[...].T.astype(BF16)


def _window_attn(z, sink_rows, batch, seq):
    nq = seq // ATT_TQ
    return pl.pallas_call(
        functools.partial(_window_attn_kernel, seq=seq),
        grid=(batch, nq),
        in_specs=[
            pl.BlockSpec((ATT_TQ, GROUP_W), lambda b, i: (b * nq + i, COL_QC // GROUP_W)),
            pl.BlockSpec((seq, KV_W), lambda b, i: (b, COL_KC // KV_W)),
            pl.BlockSpec((seq, KV_W), lambda b, i: (b, COL_VC // KV_W)),
            pl.BlockSpec((N_KV, 1, ATT_M), lambda b, i: (0, 0, 0)),
        ],
        out_specs=pl.BlockSpec((ATT_TQ, GROUP_W), lambda b, i: (b * nq + i, 0)),
        out_shape=jax.ShapeDtypeStruct((batch * seq, GROUP_W), BF16),
        scratch_shapes=[
            pltpu.VMEM((N_KV, KV_W, ATT_M), BF16),
            pltpu.VMEM((GROUP_W, ATT_TQ), F32),
        ],
        compiler_params=_cparams(("arbitrary", "arbitrary")),
        name="window_attn",
    )(z, z, z, sink_rows)


def _mix_out_kernel(ab_ref, ac_ref, acp_ref, acn_ref, av_ref, avp_ref, avn_ref,
                    da_ref, dap_ref, dan_ref, dg_ref, dgp_ref, dgn_ref,
                    wa_ref, wd_ref, bd_ref, lng_ref, lnb_ref,
                    yb_ref, yc_ref, w_ref, x_ref, g_ref,
                    o_ref, pbuf, ubuf, ush, ya_s, yd_s, *, nseq):
    i = pl.program_id(0)
    keep_prev = jnp.where(i % nseq == 0, 0.0, 1.0)
    keep_next = jnp.where(i % nseq == nseq - 1, 0.0, 1.0)

    def gate(a_ref, g_ref):
        return a_ref[...].astype(F32) * jax.nn.sigmoid(g_ref[...].astype(F32))

    def prod(a_ref, b_ref):
        return a_ref[...].astype(F32) * b_ref[...].astype(F32)

    tm = MIX_TM
    ubuf[0:HALO, :] = gate(dap_ref, dgp_ref) * keep_prev
    ubuf[HALO:HALO + tm, :] = gate(da_ref, dg_ref)
    ubuf[HALO + tm:2 * HALO + tm, :] = gate(dan_ref, dgn_ref) * keep_next
    pbuf[0:HALO, :] = prod(acp_ref, avp_ref) * keep_prev
    pbuf[HALO:HALO + tm, :] = prod(ac_ref, av_ref)
    pbuf[HALO + tm:2 * HALO + tm, :] = prod(acn_ref, avn_ref) * keep_next

    rows = tm + 2 * HALO
    ubuf[rows:rows + F32_SUBLANES, :] = jnp.zeros((F32_SUBLANES, GROUP_W), F32)
    for s in range(1, F32_SUBLANES):
        ush[s - 1, :, :] = ubuf[s:s + rows, :]

    bd, lng, lnb = bd_ref[...], lng_ref[...], lnb_ref[...]

    def mix(r0):
        acc = jnp.zeros((MIX_CH, GROUP_W), F32)
        for k in range(CONF_K):
            lo = r0 + HALO - CONF_K // 2 + k
            s = lo % F32_SUBLANES
            a = lo - s
            tap = ubuf[a:a + MIX_CH, :] if s == 0 else ush[s - 1, a:a + MIX_CH, :]
            wk = wd_ref[k * F32_SUBLANES:(k + 1) * F32_SUBLANES, :]
            acc = acc + (tap.reshape(MIX_CH // F32_SUBLANES, F32_SUBLANES, GROUP_W) * wk[None]
                         ).reshape(MIX_CH, GROUP_W)
        acc = acc + bd
        mu = jnp.mean(acc, axis=-1, keepdims=True)
        xc = acc - mu
        var = jnp.mean(xc * xc, axis=-1, keepdims=True)
        y = (xc * lax.rsqrt(var + LN_EPS)) * lng + lnb
        yd_s[r0:r0 + MIX_CH, :] = (y * jax.nn.sigmoid(y)).astype(BF16)
        pa = jnp.zeros((MIX_CH, GROUP_W), F32)
        for k in range(SCONV_K):
            lo = r0 + HALO - SCONV_K // 2 + k
            pa = pa + wa_ref[k:k + 1, :] * pbuf[lo:lo + MIX_CH, :]
        ya_s[r0:r0 + MIX_CH, :] = (ab_ref[r0:r0 + MIX_CH, :].astype(F32) * pa).astype(BF16)

    def project(rows):
        for r0 in range(rows.start, rows.stop, MIX_CH):
            mix(r0)
        y = jnp.concatenate([ya_s[rows, :], yb_ref[rows, :], yc_ref[rows, :], yd_s[rows, :]], axis=-1)
        return jnp.dot(y, w_ref[...], preferred_element_type=F32)

    def residual(rows, acc):
        ms = jnp.mean(acc * acc, axis=-1, keepdims=True)
        o_ref[rows, :] = x_ref[rows, :] + (acc * lax.rsqrt(ms + RMS_EPS)) * g_ref[...]

    _pipelined(_row_chunks(tm, OUT_RC), project, residual)


def _mix_out(z, yb, yc, seq, wa, wd, bd, lng, lnb, w_bf, x2d, g):
    t = z.shape[0]
    tm = MIX_TM
    nseq = seq // tm
    hb = tm // HALO
    nh = t // HALO

    def cur(col):
        return pl.BlockSpec((tm, GROUP_W), lambda i: (i, col // GROUP_W))

    def prev(col):
        return pl.BlockSpec((HALO, GROUP_W), lambda i: (jnp.maximum(i * hb - 1, 0), col // GROUP_W))

    def nxt(col):
        return pl.BlockSpec((HALO, GROUP_W), lambda i: (jnp.minimum((i + 1) * hb, nh - 1), col // GROUP_W))

    const = lambda shape: pl.BlockSpec(shape, lambda i: (0, 0))
    halo_cols = (COL_AC, COL_AV, COL_DA, COL_DG)
    in_specs = [cur(COL_AB)]
    for col in halo_cols:
        in_specs += [cur(col), prev(col), nxt(col)]
    in_specs += [const((SCONV_K, GROUP_W)), const((CONF_K * F32_SUBLANES, GROUP_W)), const((1, GROUP_W)),
                 const((1, GROUP_W)), const((1, GROUP_W))]
    yspec = pl.BlockSpec((tm, GROUP_W), lambda i: (i, 0))
    xspec = pl.BlockSpec((tm, D_MODEL), lambda i: (i, 0))
    in_specs += [yspec, yspec,
                 pl.BlockSpec((D_MODEL, D_MODEL), lambda i: (0, 0), pipeline_mode=pl.Buffered(1)),
                 xspec, const((1, D_MODEL))]
    return pl.pallas_call(
        functools.partial(_mix_out_kernel, nseq=nseq),
        grid=(t // tm,),
        in_specs=in_specs,
        out_specs=xspec,
        out_shape=jax.ShapeDtypeStruct((t, D_MODEL), F32),
        scratch_shapes=[pltpu.VMEM((tm + 2 * HALO, GROUP_W), F32),
                        pltpu.VMEM((tm + 2 * HALO + F32_SUBLANES, GROUP_W), F32),
                        pltpu.VMEM((F32_SUBLANES - 1, tm + 2 * HALO, GROUP_W), F32),
                        pltpu.VMEM((tm, GROUP_W), BF16),
                        pltpu.VMEM((tm, GROUP_W), BF16)],
        compiler_params=_cparams(("arbitrary",)),
        name="mix_out",
    )(*([z] * 13), wa, wd, bd, lng, lnb, yb, yc, w_bf, x2d, g)


def _ffn_kernel(x_ref, g1_ref, w1_ref, w2_ref, g2_ref, o_ref, h_ref, acc_ref, *, nf):
    j = pl.program_id(1)
    chunks = _row_chunks(x_ref.shape[0], FFN_RC)

    def norm(rows):
        x = x_ref[rows, :]
        ms = jnp.mean(x * x, axis=-1, keepdims=True)
        h = ((x * lax.rsqrt(ms + RMS_EPS)) * g1_ref[...]).astype(BF16)
        h_ref[rows, :] = h
        return h

    def mlp(h):
        u = jnp.maximum(jnp.dot(h, w1_ref[...], preferred_element_type=F32), 0.0)
        return jnp.dot((u * u).astype(BF16), w2_ref[...], preferred_element_type=F32)

    def first(rows, h):
        acc_ref[rows, :] = mlp(h)

    def last(rows, y):
        ms = jnp.mean(y * y, axis=-1, keepdims=True)
        o_ref[rows, :] = x_ref[rows, :] + (y * lax.rsqrt(ms + RMS_EPS)) * g2_ref[...]

    @pl.when(j == 0)
    def _():
        _pipelined(chunks, norm, first)

    @pl.when(jnp.logical_and(j > 0, j < nf - 1))
    def _():
        acc_ref[...] += mlp(h_ref[...])

    @pl.when(j == nf - 1)
    def _():
        _pipelined(chunks, lambda rows: acc_ref[rows, :] + mlp(h_ref[rows, :]), last)


def _ffn(x2d, g1, w1_bf, w2_bf, g2):
    t = x2d.shape[0]
    tm, tf = FFN_TM, FFN_TF
    nf = D_FF // tf
    xspec = pl.BlockSpec((tm, D_MODEL), lambda i, j: (i, 0))
    gspec = pl.BlockSpec((1, D_MODEL), lambda i, j: (0, 0))
    return pl.pallas_call(
        functools.partial(_ffn_kernel, nf=nf),
        grid=(t // tm, nf),
        in_specs=[xspec, gspec,
                  pl.BlockSpec((D_MODEL, tf), lambda i, j: (0, j)),
                  pl.BlockSpec((tf, D_MODEL), lambda i, j: (j, 0)),
                  gspec],
        out_specs=xspec,
        out_shape=jax.ShapeDtypeStruct((t, D_MODEL), F32),
        scratch_shapes=[pltpu.VMEM((tm, D_MODEL), BF16), pltpu.VMEM((tm, D_MODEL), F32)],
        compiler_params=_cparams(("arbitrary", "arbitrary")),
        name="ffn",
    )(x2d, g1, w1_bf, w2_bf, g2)


def _rope_tables(seq):
    lane = np.arange(LANES)
    t = jnp.arange(seq)
    inv_ax = 1.0 / (ROPE_THETA ** (jnp.arange(0, 32, 2, dtype=F32) / 32))
    pos_row = (t // GRID_W).astype(F32)
    pos_col = (t % GRID_W).astype(F32)
    use_col = ((lane % HEAD_DIM) >= 32)
    pos = jnp.where(use_col[None, :], pos_col[:, None], pos_row[:, None])
    ang = pos * inv_ax[lane % 16][None, :]
    sign_ax = np.where((lane % 32) >= 16, 1.0, -1.0).astype(np.float32)
    cax, sax = jnp.cos(ang), jnp.sin(ang) * sign_ax[None, :]
    inv_li = 1.0 / (ROPE_THETA ** (jnp.arange(0, HEAD_DIM, 2, dtype=F32) / HEAD_DIM))
    ang = t.astype(F32)[:, None] * inv_li[lane % 32][None, :]
    sign_li = np.where((lane % HEAD_DIM) >= 32, 1.0, -1.0).astype(np.float32)
    cli, sli = jnp.cos(ang), jnp.sin(ang) * sign_li[None, :]
    return cax, sax, cli, sli


def _permute_w_in(w):
    o = _ORIG
    order = [("ab", 512), ("ac", 512), ("av", 512), ("da", 512), ("dg", 512), ("qb", 512), ("qc", 512),
             ("kb", 128), ("vb", 128), ("kc", 128), ("vc", 128)]
    return jnp.concatenate([w[:, o[n]:o[n] + wd] for n, wd in order], axis=1).astype(BF16)


def _layer(x2d, batch, seq, tabs, ones_bd, p):
    z = _in_proj(x2d, seq, p["pre_mix_g"], p["w_in"], tabs, p["q_norm_g"], p["k_norm_g"], ones_bd)
    yb = _global_attn(z, batch, seq)
    yc = _window_attn(z, p["sink_rows"], batch, seq)
    x1 = _mix_out(z, yb, yc, seq, p["conv_a_w"], p["conv_d_w"], p["conv_d_b"], p["ln_d_g"], p["ln_d_b"],
                  p["w_out"], x2d, p["post_mix_g"])
    return _ffn(x1, p["pre_ffn_g"], p["w_ff1"], p["w_ff2"], p["post_ffn_g"])


def kernel(x_prompt, x_sample, pre_mix_g, w_in, conv_a_w, q_norm_g, k_norm_g, sink_c, conv_d_w, conv_d_b,
           ln_d_g, ln_d_b, w_out, post_mix_g, pre_ffn_g, w_ff1, w_ff2, post_ffn_g):
    depth = w_in.shape[0]
    lane = np.arange(LANES)
    ones_bd = jnp.asarray((lane[:, None] // HEAD_DIM) == (lane[None, :] // HEAD_DIM), BF16)
    layers = []
    for l in range(depth):
        row = lambda a: a[l].reshape(1, -1).astype(F32)
        sink_rows = jnp.repeat((sink_c[l].astype(F32) * LOG2_E).reshape(N_KV, 1, N_REP), ATT_TQ, axis=2)
        layers.append(dict(
            pre_mix_g=row(pre_mix_g), w_in=_permute_w_in(w_in[l]),
            conv_a_w=conv_a_w[l].astype(F32),
            conv_d_w=jnp.repeat(conv_d_w[l].astype(F32), F32_SUBLANES, axis=0),
            conv_d_b=row(conv_d_b), ln_d_g=row(ln_d_g), ln_d_b=row(ln_d_b),
            q_norm_g=jnp.tile(row(q_norm_g), (1, 2)), k_norm_g=jnp.tile(row(k_norm_g), (1, 2)),
            sink_rows=sink_rows, w_out=w_out[l].astype(BF16), post_mix_g=row(post_mix_g),
            pre_ffn_g=row(pre_ffn_g), w_ff1=w_ff1[l].astype(BF16), w_ff2=w_ff2[l].astype(BF16),
            post_ffn_g=row(post_ffn_g)))
    outs = []
    for x in (x_prompt, x_sample):
        batch, seq, _ = x.shape
        tabs = _rope_tables(seq)
        y = x.reshape(batch * seq, D_MODEL)
        for p in layers:
            y = _layer(y, batch, seq, tabs, ones_bd, p)
        outs.append(y.reshape(batch, seq, D_MODEL))
    return tuple(outs)
```

```python
import functools
import math

import jax
import jax.numpy as jnp
import numpy as np
from jax import lax
from jax.experimental import pallas as pl
from jax.experimental.pallas import tpu as pltpu

F32 = jnp.float32
BF16 = jnp.bfloat16

D_MODEL = 2048
GROUP_W = 512
HEAD_DIM = 64
N_HEADS = 8
N_KV = 2
N_REP = 4
KV_W = N_KV * HEAD_DIM
SCONV_K = 3
CONF_K = 31
WINDOW = 128
GRID_W = 64
ROPE_THETA = 10000.0
D_FF = 4 * D_MODEL
IN_W = 4096
RMS_EPS = 1e-6
LN_EPS = 1e-5
NEG_BIG = -1e30
LOG2_E = math.log2(math.e)

LANES = 128
BF16_SUBLANES = 16
F32_SUBLANES = 8
VMEM_LIMIT_BYTES = 56 * 1024 * 1024
CAST_BLOCK_BYTES = 8 * 1024 * 1024

COL_AB, COL_AC, COL_AV, COL_DA, COL_DG, COL_QB, COL_QC = (0, 512, 1024, 1536, 2048, 2560, 3072)
COL_KB, COL_VB, COL_KC, COL_VC = (3584, 3712, 3840, 3968)
_ORIG = dict(ab=0, ac=512, av=1024, qb=1536, kb=2048, vb=2176, qc=2304, kc=2816, vc=2944, da=3072, dg=3584)

IN_TN = 1024
IN_RC = 256
ATT_TQ = 256
ATT_TK = 512
ATT_M = N_REP * ATT_TQ
ATT_CW = 256
ATT_UNROLL = 8
WIN_TK = ATT_TQ + 2 * WINDOW
WIN_TILES = 2
V_ROWS = HEAD_DIM + BF16_SUBLANES
MIX_TM = 512
MIX_CH = 64
HALO = 16
OUT_RC = 256
FFN_TM = 512
FFN_TF = 1024
FFN_RC = 256


def _cparams(sem):
    return pltpu.CompilerParams(dimension_semantics=sem, vmem_limit_bytes=VMEM_LIMIT_BYTES)


def _row_chunks(rows, chunk):
    return [slice(r * chunk, (r + 1) * chunk) for r in range(rows // chunk)]


def _pipelined(chunks, produce, consume):
    nxt = produce(chunks[0])
    for n, rows in enumerate(chunks):
        cur = nxt
        if n + 1 < len(chunks):
            nxt = produce(chunks[n + 1])
        consume(rows, cur)


def _rope(x, cos, sin_signed, half, use_fwd):
    partner = jnp.where(use_fwd, pltpu.roll(x, half, 1), pltpu.roll(x, LANES - half, 1))
    return x * cos + partner * sin_signed


def _head_rms(x, gain, ones_bd):
    sq = x * x
    hi = sq.astype(BF16)
    lo = (sq - hi.astype(F32)).astype(BF16)
    ssum = (jnp.dot(hi, ones_bd, preferred_element_type=F32)
            + jnp.dot(lo, ones_bd, preferred_element_type=F32))
    return (x * lax.rsqrt(ssum * (1.0 / HEAD_DIM) + RMS_EPS)) * gain


def _in_proj_kernel(x_ref, g_ref, w_ref, cax_ref, sax_ref, cli_ref, sli_ref, qg_ref, kg_ref,
                    ones_ref, z_ref, h_ref):
    j = pl.program_id(1)
    chunks = _row_chunks(x_ref.shape[0], IN_RC)
    lane = lax.broadcasted_iota(jnp.int32, (1, LANES), 1)
    fwd_ax = (lane % 32) >= 16
    fwd_li = (lane % 64) >= 32

    def norm(rows):
        x = x_ref[rows, :]
        ms = jnp.mean(x * x, axis=-1, keepdims=True)
        h = ((x * lax.rsqrt(ms + RMS_EPS)) * g_ref[...]).astype(BF16)
        h_ref[rows, :] = h
        return h

    def mm(h):
        return jnp.dot(h, w_ref[...], preferred_element_type=F32)

    def store_plain(rows, h):
        z_ref[rows, :] = mm(h).astype(BF16)

    def store_dg_qb(rows, acc):
        z_ref[rows, 0:512] = acc[:, 0:512].astype(BF16)
        cax, sax = cax_ref[rows, :], sax_ref[rows, :]
        for c in range(4):
            lo = 512 + c * LANES
            x = _head_rms(acc[:, lo:lo + LANES], qg_ref[...], ones_ref[...])
            x = _rope(x, cax, sax, 16, fwd_ax) * (LOG2_E / math.sqrt(HEAD_DIM))
            z_ref[rows, lo:lo + LANES] = x.astype(BF16)

    def store_qc_kv(rows, acc):
        cax, sax = cax_ref[rows, :], sax_ref[rows, :]
        cli, sli = cli_ref[rows, :], sli_ref[rows, :]
        for c in range(4):
            lo = c * LANES
            x = _rope(acc[:, lo:lo + LANES], cli, sli, 32, fwd_li) * (LOG2_E / math.sqrt(HEAD_DIM))
            z_ref[rows, lo:lo + LANES] = x.astype(BF16)
        kb = _rope(_head_rms(acc[:, 512:640], kg_ref[...], ones_ref[...]), cax, sax, 16, fwd_ax)
        z_ref[rows, 512:640] = kb.astype(BF16)
        z_ref[rows, 640:768] = acc[:, 640:768].astype(BF16)
        z_ref[rows, 768:896] = _rope(acc[:, 768:896], cli, sli, 32, fwd_li).astype(BF16)
        z_ref[rows, 896:1024] = acc[:, 896:1024].astype(BF16)

    @pl.when(j == 0)
    def _():
        _pipelined(chunks, norm, store_plain)

    @pl.when(j == 1)
    def _():
        z_ref[...] = mm(h_ref[...]).astype(BF16)

    @pl.when(j == 2)
    def _():
        _pipelined(chunks, lambda rows: mm(h_ref[rows, :]), store_dg_qb)

    @pl.when(j == 3)
    def _():
        _pipelined(chunks, lambda rows: mm(h_ref[rows, :]), store_qc_kv)


def _in_proj(x2d, seq, g, w_bf, tabs, qg, kg, ones_bd):
    t = x2d.shape[0]
    tm = min(1024, seq)
    nseq = seq // tm
    row_spec = lambda width: pl.BlockSpec((tm, width), lambda i, j: (i % nseq, 0))
    const = lambda shape: pl.BlockSpec(shape, lambda i, j: (0, 0))
    return pl.pallas_call(
        _in_proj_kernel,
        grid=(t // tm, IN_W // IN_TN),
        in_specs=[
            pl.BlockSpec((tm, D_MODEL), lambda i, j: (i, 0)),
            const((1, D_MODEL)),
            pl.BlockSpec((D_MODEL, IN_TN), lambda i, j: (0, j)),
            row_spec(LANES), row_spec(LANES), row_spec(LANES), row_spec(LANES),
            const((1, LANES)), const((1, LANES)), const((LANES, LANES)),
        ],
        out_specs=pl.BlockSpec((tm, IN_TN), lambda i, j: (i, j)),
        out_shape=jax.ShapeDtypeStruct((t, IN_W), BF16),
        scratch_shapes=[pltpu.VMEM((tm, D_MODEL), BF16)],
        compiler_params=_cparams(("arbitrary", "arbitrary")),
        name="in_proj",
    )(x2d, g, w_bf, tabs[0], tabs[1], tabs[2], tabs[3], qg, kg, ones_bd)


def _build_qpad(q, qpad_ref):
    qt = q.astype(F32).T
    zeros = jnp.zeros((HEAD_DIM, ATT_M), BF16)
    for kvh in range(N_KV):
        qpad_ref[kvh, (1 - kvh) * HEAD_DIM:(2 - kvh) * HEAD_DIM, :] = zeros
        for g in range(N_REP):
            h = N_REP * kvh + g
            qpad_ref[kvh, kvh * HEAD_DIM:(kvh + 1) * HEAD_DIM, g * ATT_TQ:(g + 1) * ATT_TQ] = (
                qt[h * HEAD_DIM:(h + 1) * HEAD_DIM, :].astype(BF16))


def _store_heads(o_ref, ot_ref, o_by_kvh):
    for kvh in range(N_KV):
        for g in range(N_REP):
            h = N_REP * kvh + g
            ot_ref[h * HEAD_DIM:(h + 1) * HEAD_DIM, :] = o_by_kvh[kvh][:, g * ATT_TQ:(g + 1) * ATT_TQ]
    o_ref[...] = ot_ref[...].T.astype(BF16)


def _global_attn_kernel(q_ref, k_ref, v_ref, o_ref, qpad_ref, vt_ref, m_ref, acc_ref, ot_ref, s_ref, bm_ref,
                        *, nkb):
    i = pl.program_id(1)

    @pl.when(i == 0)
    def _():
        ones = jnp.ones((BF16_SUBLANES, ATT_TK), BF16)

        def fill(c, carry):
            r0 = pl.multiple_of(c * ATT_TK, ATT_TK)
            vt = v_ref[pl.ds(r0, ATT_TK), :].astype(F32).T
            for kvh in range(N_KV):
                vt_ref[kvh, c, 0:HEAD_DIM, :] = vt[kvh * HEAD_DIM:(kvh + 1) * HEAD_DIM, :].astype(BF16)
                vt_ref[kvh, c, HEAD_DIM:V_ROWS, :] = ones
            return carry

        lax.fori_loop(0, nkb, fill, 0)

    _build_qpad(q_ref[...], qpad_ref)
    m_ref[...] = jnp.full(m_ref.shape, NEG_BIG, F32)
    acc_ref[...] = jnp.zeros(acc_ref.shape, F32)

    chunks = [(kvh, slice(j * ATT_CW, (j + 1) * ATT_CW))
              for kvh in range(N_KV) for j in range(ATT_M // ATT_CW)]

    def load_keys(c):
        r0 = pl.multiple_of(c * ATT_TK, ATT_TK)
        return k_ref[pl.ds(r0, ATT_TK), :]

    def scores(kb, slot, kvh, cols):
        s = jnp.dot(kb, qpad_ref[kvh, :, cols], preferred_element_type=F32)
        s_ref[slot, kvh, :, cols] = s
        bm_ref[slot, kvh, :, cols] = jnp.max(s, axis=0, keepdims=True)

    def softmax_pv(c, slot, kvh, cols):
        m_old = m_ref[kvh, :, cols]
        m_new = jnp.maximum(m_old, bm_ref[slot, kvh, :, cols])
        alpha = jnp.exp2(m_old - m_new)
        p = jnp.exp2(s_ref[slot, kvh, :, cols] - m_new).astype(BF16)
        pv = jnp.dot(vt_ref[kvh, c], p, preferred_element_type=F32)
        acc_ref[kvh, :, cols] = acc_ref[kvh, :, cols] * alpha + pv
        m_ref[kvh, :, cols] = m_new

    kb0 = load_keys(0)
    for kvh, cols in chunks:
        scores(kb0, 0, kvh, cols)

    unroll = math.gcd(nkb, ATT_UNROLL)

    def body(it, carry):
        for u in range(unroll):
            c = unroll * it + u
            slot = u % 2
            kb = load_keys(jnp.minimum(c + 1, nkb - 1))
            for kvh, cols in chunks:
                scores(kb, 1 - slot, kvh, cols)
                softmax_pv(c, slot, kvh, cols)
        return carry

    lax.fori_loop(0, nkb // unroll, body, 0)

    outs = []
    for kvh in range(N_KV):
        acc = acc_ref[kvh]
        outs.append(acc[0:HEAD_DIM, :] / acc[HEAD_DIM:HEAD_DIM + 1, :])
    _store_heads(o_ref, ot_ref, outs)


def _global_attn(z, batch, seq):
    nq = seq // ATT_TQ
    nkb = seq // ATT_TK
    assert seq % (2 * ATT_TK) == 0, "the two score slots alternate: an even number of key blocks is required"
    return pl.pallas_call(
        functools.partial(_global_attn_kernel, nkb=nkb),
        grid=(batch, nq),
        in_specs=[
            pl.BlockSpec((ATT_TQ, GROUP_W), lambda b, i: (b * nq + i, COL_QB // GROUP_W)),
            pl.BlockSpec((seq, KV_W), lambda b, i: (b, COL_KB // KV_W)),
            pl.BlockSpec((seq, KV_W), lambda b, i: (b, COL_VB // KV_W)),
        ],
        out_specs=pl.BlockSpec((ATT_TQ, GROUP_W), lambda b, i: (b * nq + i, 0)),
        out_shape=jax.ShapeDtypeStruct((batch * seq, GROUP_W), BF16),
        scratch_shapes=[
            pltpu.VMEM((N_KV, KV_W, ATT_M), BF16),
            pltpu.VMEM((N_KV, nkb, V_ROWS, ATT_TK), BF16),
            pltpu.VMEM((N_KV, 1, ATT_M), F32),
            pltpu.VMEM((N_KV, V_ROWS, ATT_M), F32),
            pltpu.VMEM((GROUP_W, ATT_TQ), F32),
            pltpu.VMEM((2, N_KV, ATT_TK, ATT_M), F32),
            pltpu.VMEM((2, N_KV, 1, ATT_M), F32),
        ],
        compiler_params=_cparams(("arbitrary", "arbitrary")),
        name="global_attn",
    )(z, z, z)


def _window_attn_kernel(q_ref, k_ref, v_ref, sink_ref, o_ref, qpad_ref, ot_ref, *, seq):
    i = pl.program_id(1)
    ones = jnp.ones((BF16_SUBLANES, WIN_TK), BF16)
    tiles = []
    for t in range(WIN_TILES):
        _build_qpad(q_ref[t * ATT_TQ:(t + 1) * ATT_TQ, :], qpad_ref.at[t])
        q0 = (i * WIN_TILES + t) * ATT_TQ
        start = pl.multiple_of(jnp.clip(q0 - WINDOW, 0, seq - WIN_TK), WINDOW)
        kb = k_ref[pl.ds(start, WIN_TK), :]
        vt = v_ref[pl.ds(start, WIN_TK), :].astype(F32).T
        kpos = start + lax.broadcasted_iota(jnp.int32, (WIN_TK, ATT_TQ), 0)
        qpos = q0 + lax.broadcasted_iota(jnp.int32, (WIN_TK, ATT_TQ), 1)
        valid = jnp.abs(qpos - kpos) <= WINDOW
        vks = [jnp.concatenate([vt[kvh * HEAD_DIM:(kvh + 1) * HEAD_DIM, :].astype(BF16), ones], axis=0)
               for kvh in range(N_KV)]
        tiles.append((kb, vks, valid))
    work = [(t, kvh, g) for kvh in range(N_KV) for g in range(N_REP) for t in range(WIN_TILES)]

    def scores(t, kvh, g):
        kb, _, valid = tiles[t]
        s = jnp.dot(kb, qpad_ref[t, kvh, :, g * ATT_TQ:(g + 1) * ATT_TQ], preferred_element_type=F32)
        return jnp.where(valid, s, NEG_BIG)

    def finish(t, kvh, g, s):
        sk = sink_ref[kvh, :, g * ATT_TQ:(g + 1) * ATT_TQ]
        m = jnp.maximum(jnp.max(s, axis=0, keepdims=True), sk)
        p = jnp.exp2(s - m).astype(BF16)
        o = jnp.dot(tiles[t][1][kvh], p, preferred_element_type=F32)
        denom = o[HEAD_DIM:HEAD_DIM + 1, :] + jnp.exp2(sk - m)
        h = N_REP * kvh + g
        ot_ref[t, h * HEAD_DIM:(h + 1) * HEAD_DIM, :] = o[0:HEAD_DIM, :] / denom

    s_next = scores(*work[0])
    for n, item in enumerate(work):
        s_cur = s_next
        if n + 1 < len(work):
            s_next = scores(*work[n + 1])
        finish(*item, s_cur)
    for t in range(WIN_TILES):
        o_ref[t * ATT_TQ:(t + 1) * ATT_TQ, :] = ot_ref[t].T.astype(BF16)


def _window_attn(z, sink_rows, batch, seq):
    rows = WIN_TILES * ATT_TQ
    nq = seq // rows
    return pl.pallas_call(
        functools.partial(_window_attn_kernel, seq=seq),
        grid=(batch, nq),
        in_specs=[
            pl.BlockSpec((rows, GROUP_W), lambda b, i: (b * nq + i, COL_QC // GROUP_W)),
            pl.BlockSpec((seq, KV_W), lambda b, i: (b, COL_KC // KV_W)),
            pl.BlockSpec((seq, KV_W), lambda b, i: (b, COL_VC // KV_W)),
            pl.BlockSpec((N_KV, 1, ATT_M), lambda b, i: (0, 0, 0)),
        ],
        out_specs=pl.BlockSpec((rows, GROUP_W), lambda b, i: (b * nq + i, 0)),
        out_shape=jax.ShapeDtypeStruct((batch * seq, GROUP_W), BF16),
        scratch_shapes=[
            pltpu.VMEM((WIN_TILES, N_KV, KV_W, ATT_M), BF16),
            pltpu.VMEM((WIN_TILES, GROUP_W, ATT_TQ), F32),
        ],
        compiler_params=_cparams(("arbitrary", "arbitrary")),
        name="window_attn",
    )(z, z, z, sink_rows)


def _mix_out_kernel(ab_ref, ac_ref, acp_ref, acn_ref, av_ref, avp_ref, avn_ref,
                    da_ref, dap_ref, dan_ref, dg_ref, dgp_ref, dgn_ref,
                    wa_ref, wd_ref, bd_ref, lng_ref, lnb_ref,
                    yb_ref, yc_ref, w_ref, x_ref, g_ref,
                    o_ref, pbuf, ubuf, ush, ya_s, yd_s, *, nseq):
    i = pl.program_id(0)
    keep_prev = jnp.where(i % nseq == 0, 0.0, 1.0)
    keep_next = jnp.where(i % nseq == nseq - 1, 0.0, 1.0)

    def gate(a_ref, g_ref):
        return a_ref[...].astype(F32) * jax.nn.sigmoid(g_ref[...].astype(F32))

    def prod(a_ref, b_ref):
        return a_ref[...].astype(F32) * b_ref[...].astype(F32)

    tm = MIX_TM
    ubuf[0:HALO, :] = gate(dap_ref, dgp_ref) * keep_prev
    ubuf[HALO:HALO + tm, :] = gate(da_ref, dg_ref)
    ubuf[HALO + tm:2 * HALO + tm, :] = gate(dan_ref, dgn_ref) * keep_next
    pbuf[0:HALO, :] = prod(acp_ref, avp_ref) * keep_prev
    pbuf[HALO:HALO + tm, :] = prod(ac_ref, av_ref)
    pbuf[HALO + tm:2 * HALO + tm, :] = prod(acn_ref, avn_ref) * keep_next

    rows = tm + 2 * HALO
    ubuf[rows:rows + F32_SUBLANES, :] = jnp.zeros((F32_SUBLANES, GROUP_W), F32)
    for s in range(1, F32_SUBLANES):
        ush[s - 1, :, :] = ubuf[s:s + rows, :]

    bd, lng, lnb = bd_ref[...], lng_ref[...], lnb_ref[...]

    def mix(r0):
        acc = jnp.zeros((MIX_CH, GROUP_W), F32)
        for k in range(CONF_K):
            lo = r0 + HALO - CONF_K // 2 + k
            s = lo % F32_SUBLANES
            a = lo - s
            tap = ubuf[a:a + MIX_CH, :] if s == 0 else ush[s - 1, a:a + MIX_CH, :]
            wk = wd_ref[k * F32_SUBLANES:(k + 1) * F32_SUBLANES, :]
            acc = acc + (tap.reshape(MIX_CH // F32_SUBLANES, F32_SUBLANES, GROUP_W) * wk[None]
                         ).reshape(MIX_CH, GROUP_W)
        acc = acc + bd
        mu = jnp.mean(acc, axis=-1, keepdims=True)
        xc = acc - mu
        var = jnp.mean(xc * xc, axis=-1, keepdims=True)
        y = (xc * lax.rsqrt(var + LN_EPS)) * lng + lnb
        yd_s[r0:r0 + MIX_CH, :] = (y * jax.nn.sigmoid(y)).astype(BF16)
        pa = jnp.zeros((MIX_CH, GROUP_W), F32)
        for k in range(SCONV_K):
            lo = r0 + HALO - SCONV_K // 2 + k
            pa = pa + wa_ref[k:k + 1, :] * pbuf[lo:lo + MIX_CH, :]
        ya_s[r0:r0 + MIX_CH, :] = (ab_ref[r0:r0 + MIX_CH, :].astype(F32) * pa).astype(BF16)

    def project(rows):
        for r0 in range(rows.start, rows.stop, MIX_CH):
            mix(r0)
        y = jnp.concatenate([ya_s[rows, :], yb_ref[rows, :], yc_ref[rows, :], yd_s[rows, :]], axis=-1)
        return jnp.dot(y, w_ref[...], preferred_element_type=F32)

    def residual(rows, acc):
        ms = jnp.mean(acc * acc, axis=-1, keepdims=True)
        o_ref[rows, :] = x_ref[rows, :] + (acc * lax.rsqrt(ms + RMS_EPS)) * g_ref[...]

    _pipelined(_row_chunks(tm, OUT_RC), project, residual)


def _mix_out(z, yb, yc, seq, wa, wd, bd, lng, lnb, w_bf, x2d, g):
    t = z.shape[0]
    tm = MIX_TM
    nseq = seq // tm
    hb = tm // HALO
    nh = t // HALO

    def cur(col):
        return pl.BlockSpec((tm, GROUP_W), lambda i: (i, col // GROUP_W))

    def prev(col):
        return pl.BlockSpec((HALO, GROUP_W), lambda i: (jnp.maximum(i * hb - 1, 0), col // GROUP_W))

    def nxt(col):
        return pl.BlockSpec((HALO, GROUP_W), lambda i: (jnp.minimum((i + 1) * hb, nh - 1), col // GROUP_W))

    const = lambda shape: pl.BlockSpec(shape, lambda i: (0, 0))
    halo_cols = (COL_AC, COL_AV, COL_DA, COL_DG)
    in_specs = [cur(COL_AB)]
    for col in halo_cols:
        in_specs += [cur(col), prev(col), nxt(col)]
    in_specs += [const((SCONV_K, GROUP_W)), const((CONF_K * F32_SUBLANES, GROUP_W)), const((1, GROUP_W)),
                 const((1, GROUP_W)), const((1, GROUP_W))]
    yspec = pl.BlockSpec((tm, GROUP_W), lambda i: (i, 0))
    xspec = pl.BlockSpec((tm, D_MODEL), lambda i: (i, 0))
    in_specs += [yspec, yspec,
                 pl.BlockSpec((D_MODEL, D_MODEL), lambda i: (0, 0), pipeline_mode=pl.Buffered(1)),
                 xspec, const((1, D_MODEL))]
    return pl.pallas_call(
        functools.partial(_mix_out_kernel, nseq=nseq),
        grid=(t // tm,),
        in_specs=in_specs,
        out_specs=xspec,
        out_shape=jax.ShapeDtypeStruct((t, D_MODEL), F32),
        scratch_shapes=[pltpu.VMEM((tm + 2 * HALO, GROUP_W), F32),
                        pltpu.VMEM((tm + 2 * HALO + F32_SUBLANES, GROUP_W), F32),
                        pltpu.VMEM((F32_SUBLANES - 1, tm + 2 * HALO, GROUP_W), F32),
                        pltpu.VMEM((tm, GROUP_W), BF16),
                        pltpu.VMEM((tm, GROUP_W), BF16)],
        compiler_params=_cparams(("arbitrary",)),
        name="mix_out",
    )(*([z] * 13), wa, wd, bd, lng, lnb, yb, yc, w_bf, x2d, g)


def _ffn_kernel(x_ref, g1_ref, w1_ref, w2_ref, g2_ref, o_ref, h_ref, acc_ref, *, nf):
    j = pl.program_id(1)
    chunks = _row_chunks(x_ref.shape[0], FFN_RC)

    def norm(rows):
        x = x_ref[rows, :]
        ms = jnp.mean(x * x, axis=-1, keepdims=True)
        h = ((x * lax.rsqrt(ms + RMS_EPS)) * g1_ref[...]).astype(BF16)
        h_ref[rows, :] = h
        return h

    def mlp(h):
        u = jnp.maximum(jnp.dot(h, w1_ref[...], preferred_element_type=F32), 0.0)
        return jnp.dot((u * u).astype(BF16), w2_ref[...], preferred_element_type=F32)

    def first(rows, h):
        acc_ref[rows, :] = mlp(h)

    def last(rows, y):
        ms = jnp.mean(y * y, axis=-1, keepdims=True)
        o_ref[rows, :] = x_ref[rows, :] + (y * lax.rsqrt(ms + RMS_EPS)) * g2_ref[...]

    @pl.when(j == 0)
    def _():
        _pipelined(chunks, norm, first)

    @pl.when(jnp.logical_and(j > 0, j < nf - 1))
    def _():
        acc_ref[...] += mlp(h_ref[...])

    @pl.when(j == nf - 1)
    def _():
        _pipelined(chunks, lambda rows: acc_ref[rows, :] + mlp(h_ref[rows, :]), last)


def _ffn(x2d, g1, w1_bf, w2_bf, g2):
    t = x2d.shape[0]
    tm, tf = FFN_TM, FFN_TF
    nf = D_FF // tf
    xspec = pl.BlockSpec((tm, D_MODEL), lambda i, j: (i, 0))
    gspec = pl.BlockSpec((1, D_MODEL), lambda i, j: (0, 0))
    return pl.pallas_call(
        functools.partial(_ffn_kernel, nf=nf),
        grid=(t // tm, nf),
        in_specs=[xspec, gspec,
                  pl.BlockSpec((D_MODEL, tf), lambda i, j: (0, j)),
                  pl.BlockSpec((tf, D_MODEL), lambda i, j: (j, 0)),
                  gspec],
        out_specs=xspec,
        out_shape=jax.ShapeDtypeStruct((t, D_MODEL), F32),
        scratch_shapes=[pltpu.VMEM((tm, D_MODEL), BF16), pltpu.VMEM((tm, D_MODEL), F32)],
        compiler_params=_cparams(("arbitrary", "arbitrary")),
        name="ffn",
    )(x2d, g1, w1_bf, w2_bf, g2)


def _rope_tables(seq):
    lane = np.arange(LANES)
    t = jnp.arange(seq)
    inv_ax = 1.0 / (ROPE_THETA ** (jnp.arange(0, 32, 2, dtype=F32) / 32))
    pos_row = (t // GRID_W).astype(F32)
    pos_col = (t % GRID_W).astype(F32)
    use_col = ((lane % HEAD_DIM) >= 32)
    pos = jnp.where(use_col[None, :], pos_col[:, None], pos_row[:, None])
    ang = pos * inv_ax[lane % 16][None, :]
    sign_ax = np.where((lane % 32) >= 16, 1.0, -1.0).astype(np.float32)
    cax, sax = jnp.cos(ang), jnp.sin(ang) * sign_ax[None, :]
    inv_li = 1.0 / (ROPE_THETA ** (jnp.arange(0, HEAD_DIM, 2, dtype=F32) / HEAD_DIM))
    ang = t.astype(F32)[:, None] * inv_li[lane % 32][None, :]
    sign_li = np.where((lane % HEAD_DIM) >= 32, 1.0, -1.0).astype(np.float32)
    cli, sli = jnp.cos(ang), jnp.sin(ang) * sign_li[None, :]
    return cax, sax, cli, sli


def _cast_kernel(x_ref, o_ref):
    o_ref[...] = x_ref[...].astype(BF16)


def _to_bf16(w, layer):
    _, r, c = w.shape
    br = min(r, CAST_BLOCK_BYTES // (4 * c))
    assert r % br == 0 and br % BF16_SUBLANES == 0
    return pl.pallas_call(
        _cast_kernel,
        grid=(r // br,),
        in_specs=[pl.BlockSpec((None, br, c), lambda i: (layer, i, 0))],
        out_specs=pl.BlockSpec((br, c), lambda i: (i, 0)),
        out_shape=jax.ShapeDtypeStruct((r, c), BF16),
        compiler_params=_cparams(("arbitrary",)),
        name="cast_bf16",
    )(w)


def _permute_w_in(w):
    o = _ORIG
    order = [("ab", 512), ("ac", 512), ("av", 512), ("da", 512), ("dg", 512), ("qb", 512), ("qc", 512),
             ("kb", 128), ("vb", 128), ("kc", 128), ("vc", 128)]
    return jnp.concatenate([w[:, o[n]:o[n] + wd] for n, wd in order], axis=1).astype(BF16)


def _layer(x2d, batch, seq, tabs, ones_bd, p):
    z = _in_proj(x2d, seq, p["pre_mix_g"], p["w_in"], tabs, p["q_norm_g"], p["k_norm_g"], ones_bd)
    yb = _global_attn(z, batch, seq)
    yc = _window_attn(z, p["sink_rows"], batch, seq)
    x1 = _mix_out(z, yb, yc, seq, p["conv_a_w"], p["conv_d_w"], p["conv_d_b"], p["ln_d_g"], p["ln_d_b"],
                  p["w_out"], x2d, p["post_mix_g"])
    return _ffn(x1, p["pre_ffn_g"], p["w_ff1"], p["w_ff2"], p["post_ffn_g"])


def kernel(x_prompt, x_sample, pre_mix_g, w_in, conv_a_w, q_norm_g, k_norm_g, sink_c, conv_d_w, conv_d_b,
           ln_d_g, ln_d_b, w_out, post_mix_g, pre_ffn_g, w_ff1, w_ff2, post_ffn_g):
    depth = w_in.shape[0]
    lane = np.arange(LANES)
    ones_bd = jnp.asarray((lane[:, None] // HEAD_DIM) == (lane[None, :] // HEAD_DIM), BF16)
    layers = []
    for l in range(depth):
        row = lambda a: a[l].reshape(1, -1).astype(F32)
        sink_rows = jnp.repeat((sink_c[l].astype(F32) * LOG2_E).reshape(N_KV, 1, N_REP), ATT_TQ, axis=2)
        layers.append(dict(
            pre_mix_g=row(pre_mix_g), w_in=_permute_w_in(w_in[l]),
            conv_a_w=conv_a_w[l].astype(F32),
            conv_d_w=jnp.repeat(conv_d_w[l].astype(F32), F32_SUBLANES, axis=0),
            conv_d_b=row(conv_d_b), ln_d_g=row(ln_d_g), ln_d_b=row(ln_d_b),
            q_norm_g=jnp.tile(row(q_norm_g), (1, 2)), k_norm_g=jnp.tile(row(k_norm_g), (1, 2)),
            sink_rows=sink_rows, w_out=_to_bf16(w_out, l), post_mix_g=row(post_mix_g),
            pre_ffn_g=row(pre_ffn_g), w_ff1=_to_bf16(w_ff1, l), w_ff2=_to_bf16(w_ff2, l),
            post_ffn_g=row(post_ffn_g)))
    outs = []
    for x in (x_prompt, x_sample):
        batch, seq, _ = x.shape
        tabs = _rope_tables(seq)
        y = x.reshape(batch * seq, D_MODEL)
        for p in layers:
            y = _layer(y, batch, seq, tabs, ones_bd, p)
        outs.append(y.reshape(batch, seq, D_MODEL))
    return tuple(outs)
```

```python
import functools
import math

import jax
import jax.numpy as jnp
import numpy as np
from jax import lax
from jax.experimental import pallas as pl
from jax.experimental.pallas import tpu as pltpu

F32 = jnp.float32
BF16 = jnp.bfloat16

D_MODEL = 2048
GROUP_W = 512
HEAD_DIM = 64
N_HEADS = 8
N_KV = 2
N_REP = 4
KV_W = N_KV * HEAD_DIM
SCONV_K = 3
CONF_K = 31
WINDOW = 128
GRID_W = 64
ROPE_THETA = 10000.0
D_FF = 4 * D_MODEL
IN_W = 4096
RMS_EPS = 1e-6
LN_EPS = 1e-5
NEG_BIG = -1e30
LOG2_E = math.log2(math.e)

LANES = 128
BF16_SUBLANES = 16
F32_SUBLANES = 8
VMEM_LIMIT_BYTES = 56 * 1024 * 1024
CAST_BLOCK_BYTES = 8 * 1024 * 1024

COL_P, COL_U, COL_AB, COL_QB, COL_QC = (0, 512, 1024, 1536, 2048)
COL_KB, COL_VB, COL_KC, COL_VC = (2560, 2688, 2816, 2944)
Z_W = 3072
_ORIG = dict(ab=0, ac=512, av=1024, qb=1536, kb=2048, vb=2176, qc=2304, kc=2816, vc=2944, da=3072, dg=3584)

IN_TN = 1024
IN_RC = 256
ATT_TQ = 256
ATT_TK = 512
ATT_M = N_REP * ATT_TQ
ATT_CW = 256
ATT_UNROLL = 8
WIN_TK = ATT_TQ + 2 * WINDOW
WIN_TILES = 2
V_ROWS = HEAD_DIM + BF16_SUBLANES
MIX_TM = 512
MIX_CH = 64
HALO = 16
OUT_RC = 256
FFN_TM = 512
FFN_TF = 1024
FFN_RC = 256


def _cparams(sem):
    return pltpu.CompilerParams(dimension_semantics=sem, vmem_limit_bytes=VMEM_LIMIT_BYTES)


def _row_chunks(rows, chunk):
    return [slice(r * chunk, (r + 1) * chunk) for r in range(rows // chunk)]


def _pipelined(chunks, produce, consume):
    nxt = produce(chunks[0])
    for n, rows in enumerate(chunks):
        cur = nxt
        if n + 1 < len(chunks):
            nxt = produce(chunks[n + 1])
        consume(rows, cur)


def _rope(x, cos, sin_signed, half, use_fwd):
    partner = jnp.where(use_fwd, pltpu.roll(x, half, 1), pltpu.roll(x, LANES - half, 1))
    return x * cos + partner * sin_signed


def _head_rms(x, gain, ones_bd):
    sq = x * x
    hi = sq.astype(BF16)
    lo = (sq - hi.astype(F32)).astype(BF16)
    ssum = (jnp.dot(hi, ones_bd, preferred_element_type=F32)
            + jnp.dot(lo, ones_bd, preferred_element_type=F32))
    return (x * lax.rsqrt(ssum * (1.0 / HEAD_DIM) + RMS_EPS)) * gain


def _in_proj_kernel(x_ref, g_ref, w_ref, cax_ref, sax_ref, cli_ref, sli_ref, qg_ref, kg_ref,
                    ones_ref, z_ref, h_ref):
    j = pl.program_id(1)
    chunks = _row_chunks(x_ref.shape[0], IN_RC)
    lane = lax.broadcasted_iota(jnp.int32, (1, LANES), 1)
    fwd_ax = (lane % 32) >= 16
    fwd_li = (lane % 64) >= 32

    def norm(rows):
        x = x_ref[rows, :]
        ms = jnp.mean(x * x, axis=-1, keepdims=True)
        h = ((x * lax.rsqrt(ms + RMS_EPS)) * g_ref[...]).astype(BF16)
        h_ref[rows, :] = h
        return h

    def mm(h):
        return jnp.dot(h, w_ref[...], preferred_element_type=F32)

    def store_p(rows, h):
        acc = mm(h)
        z_ref[rows, 0:512] = (acc[:, 0:512] * acc[:, 512:1024]).astype(BF16)

    def store_u(rows, acc):
        z_ref[rows, 512:1024] = (acc[:, 0:512] * jax.nn.sigmoid(acc[:, 512:1024])).astype(BF16)

    def store_ab_qb(rows, acc):
        z_ref[rows, 0:512] = acc[:, 0:512].astype(BF16)
        cax, sax = cax_ref[rows, :], sax_ref[rows, :]
        for c in range(4):
            lo = 512 + c * LANES
            x = _head_rms(acc[:, lo:lo + LANES], qg_ref[...], ones_ref[...])
            x = _rope(x, cax, sax, 16, fwd_ax) * (LOG2_E / math.sqrt(HEAD_DIM))
            z_ref[rows, lo:lo + LANES] = x.astype(BF16)

    def store_qc_kv(rows, acc):
        cax, sax = cax_ref[rows, :], sax_ref[rows, :]
        cli, sli = cli_ref[rows, :], sli_ref[rows, :]
        for c in range(4):
            lo = c * LANES
            x = _rope(acc[:, lo:lo + LANES], cli, sli, 32, fwd_li) * (LOG2_E / math.sqrt(HEAD_DIM))
            z_ref[rows, lo:lo + LANES] = x.astype(BF16)
        kb = _rope(_head_rms(acc[:, 512:640], kg_ref[...], ones_ref[...]), cax, sax, 16, fwd_ax)
        z_ref[rows, 512:640] = kb.astype(BF16)
        z_ref[rows, 640:768] = acc[:, 640:768].astype(BF16)
        z_ref[rows, 768:896] = _rope(acc[:, 768:896], cli, sli, 32, fwd_li).astype(BF16)
        z_ref[rows, 896:1024] = acc[:, 896:1024].astype(BF16)

    @pl.when(j == 0)
    def _():
        _pipelined(chunks, norm, store_p)

    @pl.when(j == 1)
    def _():
        _pipelined(chunks, lambda rows: mm(h_ref[rows, :]), store_u)

    @pl.when(j == 2)
    def _():
        _pipelined(chunks, lambda rows: mm(h_ref[rows, :]), store_ab_qb)

    @pl.when(j == 3)
    def _():
        _pipelined(chunks, lambda rows: mm(h_ref[rows, :]), store_qc_kv)


def _in_proj(x2d, seq, g, w_bf, tabs, qg, kg, ones_bd):
    t = x2d.shape[0]
    tm = min(1024, seq)
    nseq = seq // tm
    row_spec = lambda width: pl.BlockSpec((tm, width), lambda i, j: (i % nseq, 0))
    const = lambda shape: pl.BlockSpec(shape, lambda i, j: (0, 0))
    return pl.pallas_call(
        _in_proj_kernel,
        grid=(t // tm, IN_W // IN_TN),
        in_specs=[
            pl.BlockSpec((tm, D_MODEL), lambda i, j: (i, 0)),
            const((1, D_MODEL)),
            pl.BlockSpec((D_MODEL, IN_TN), lambda i, j: (0, j)),
            row_spec(LANES), row_spec(LANES), row_spec(LANES), row_spec(LANES),
            const((1, LANES)), const((1, LANES)), const((LANES, LANES)),
        ],
        out_specs=pl.BlockSpec((tm, IN_TN), lambda i, j: (i, jnp.maximum(j - 1, 0))),
        out_shape=jax.ShapeDtypeStruct((t, Z_W), BF16),
        scratch_shapes=[pltpu.VMEM((tm, D_MODEL), BF16)],
        compiler_params=_cparams(("arbitrary", "arbitrary")),
        name="in_proj",
    )(x2d, g, w_bf, tabs[0], tabs[1], tabs[2], tabs[3], qg, kg, ones_bd)


def _build_qpad(q, qpad_ref):
    qt = q.astype(F32).T
    zeros = jnp.zeros((HEAD_DIM, ATT_M), BF16)
    for kvh in range(N_KV):
        qpad_ref[kvh, (1 - kvh) * HEAD_DIM:(2 - kvh) * HEAD_DIM, :] = zeros
        for g in range(N_REP):
            h = N_REP * kvh + g
            qpad_ref[kvh, kvh * HEAD_DIM:(kvh + 1) * HEAD_DIM, g * ATT_TQ:(g + 1) * ATT_TQ] = (
                qt[h * HEAD_DIM:(h + 1) * HEAD_DIM, :].astype(BF16))


def _store_heads(o_ref, ot_ref, o_by_kvh):
    for kvh in range(N_KV):
        for g in range(N_REP):
            h = N_REP * kvh + g
            ot_ref[h * HEAD_DIM:(h + 1) * HEAD_DIM, :] = o_by_kvh[kvh][:, g * ATT_TQ:(g + 1) * ATT_TQ]
    o_ref[...] = ot_ref[...].T.astype(BF16)


def _global_attn_kernel(q_ref, k_ref, v_ref, o_ref, qpad_ref, vt_ref, m_ref, acc_ref, ot_ref, s_ref, bm_ref,
                        *, nkb):
    i = pl.program_id(1)

    @pl.when(i == 0)
    def _():
        ones = jnp.ones((BF16_SUBLANES, ATT_TK), BF16)

        def fill(c, carry):
            r0 = pl.multiple_of(c * ATT_TK, ATT_TK)
            vt = v_ref[pl.ds(r0, ATT_TK), :].astype(F32).T
            for kvh in range(N_KV):
                vt_ref[kvh, c, 0:HEAD_DIM, :] = vt[kvh * HEAD_DIM:(kvh + 1) * HEAD_DIM, :].astype(BF16)
                vt_ref[kvh, c, HEAD_DIM:V_ROWS, :] = ones
            return carry

        lax.fori_loop(0, nkb, fill, 0)

    _build_qpad(q_ref[...], qpad_ref)
    m_ref[...] = jnp.full(m_ref.shape, NEG_BIG, F32)
    acc_ref[...] = jnp.zeros(acc_ref.shape, F32)

    chunks = [(kvh, slice(j * ATT_CW, (j + 1) * ATT_CW))
              for kvh in range(N_KV) for j in range(ATT_M // ATT_CW)]

    def load_keys(c):
        r0 = pl.multiple_of(c * ATT_TK, ATT_TK)
        return k_ref[pl.ds(r0, ATT_TK), :]

    def scores(kb, slot, kvh, cols):
        s = jnp.dot(kb, qpad_ref[kvh, :, cols], preferred_element_type=F32)
        s_ref[slot, kvh, :, cols] = s
        bm_ref[slot, kvh, :, cols] = jnp.max(s, axis=0, keepdims=True)

    def softmax_pv(c, slot, kvh, cols):
        m_old = m_ref[kvh, :, cols]
        m_new = jnp.maximum(m_old, bm_ref[slot, kvh, :, cols])
        alpha = jnp.exp2(m_old - m_new)
        p = jnp.exp2(s_ref[slot, kvh, :, cols] - m_new).astype(BF16)
        pv = jnp.dot(vt_ref[kvh, c], p, preferred_element_type=F32)
        acc_ref[kvh, :, cols] = acc_ref[kvh, :, cols] * alpha + pv
        m_ref[kvh, :, cols] = m_new

    kb0 = load_keys(0)
    for kvh, cols in chunks:
        scores(kb0, 0, kvh, cols)

    unroll = math.gcd(nkb, ATT_UNROLL)

    def body(it, carry):
        for u in range(unroll):
            c = unroll * it + u
            slot = u % 2
            kb = load_keys(jnp.minimum(c + 1, nkb - 1))
            for kvh, cols in chunks:
                scores(kb, 1 - slot, kvh, cols)
                softmax_pv(c, slot, kvh, cols)
        return carry

    lax.fori_loop(0, nkb // unroll, body, 0)

    outs = []
    for kvh in range(N_KV):
        acc = acc_ref[kvh]
        outs.append(acc[0:HEAD_DIM, :] / acc[HEAD_DIM:HEAD_DIM + 1, :])
    _store_heads(o_ref, ot_ref, outs)


def _global_attn(z, batch, seq):
    nq = seq // ATT_TQ
    nkb = seq // ATT_TK
    assert seq % (2 * ATT_TK) == 0, "the two score slots alternate: an even number of key blocks is required"
    return pl.pallas_call(
        functools.partial(_global_attn_kernel, nkb=nkb),
        grid=(batch, nq),
        in_specs=[
            pl.BlockSpec((ATT_TQ, GROUP_W), lambda b, i: (b * nq + i, COL_QB // GROUP_W)),
            pl.BlockSpec((seq, KV_W), lambda b, i: (b, COL_KB // KV_W)),
            pl.BlockSpec((seq, KV_W), lambda b, i: (b, COL_VB // KV_W)),
        ],
        out_specs=pl.BlockSpec((ATT_TQ, GROUP_W), lambda b, i: (b * nq + i, 0)),
        out_shape=jax.ShapeDtypeStruct((batch * seq, GROUP_W), BF16),
        scratch_shapes=[
            pltpu.VMEM((N_KV, KV_W, ATT_M), BF16),
            pltpu.VMEM((N_KV, nkb, V_ROWS, ATT_TK), BF16),
            pltpu.VMEM((N_KV, 1, ATT_M), F32),
            pltpu.VMEM((N_KV, V_ROWS, ATT_M), F32),
            pltpu.VMEM((GROUP_W, ATT_TQ), F32),
            pltpu.VMEM((2, N_KV, ATT_TK, ATT_M), F32),
            pltpu.VMEM((2, N_KV, 1, ATT_M), F32),
        ],
        compiler_params=_cparams(("arbitrary", "arbitrary")),
        name="global_attn",
    )(z, z, z)


def _window_attn_kernel(q_ref, k_ref, v_ref, sink_ref, o_ref, qpad_ref, ot_ref, *, seq):
    i = pl.program_id(1)
    ones = jnp.ones((BF16_SUBLANES, WIN_TK), BF16)
    tiles = []
    for t in range(WIN_TILES):
        _build_qpad(q_ref[t * ATT_TQ:(t + 1) * ATT_TQ, :], qpad_ref.at[t])
        q0 = (i * WIN_TILES + t) * ATT_TQ
        start = pl.multiple_of(jnp.clip(q0 - WINDOW, 0, seq - WIN_TK), WINDOW)
        kb = k_ref[pl.ds(start, WIN_TK), :]
        vt = v_ref[pl.ds(start, WIN_TK), :].astype(F32).T
        kpos = start + lax.broadcasted_iota(jnp.int32, (WIN_TK, ATT_TQ), 0)
        qpos = q0 + lax.broadcasted_iota(jnp.int32, (WIN_TK, ATT_TQ), 1)
        valid = jnp.abs(qpos - kpos) <= WINDOW
        vks = [jnp.concatenate([vt[kvh * HEAD_DIM:(kvh + 1) * HEAD_DIM, :].astype(BF16), ones], axis=0)
               for kvh in range(N_KV)]
        tiles.append((kb, vks, valid))
    work = [(t, kvh, g) for kvh in range(N_KV) for g in range(N_REP) for t in range(WIN_TILES)]

    def scores(t, kvh, g):
        kb, _, valid = tiles[t]
        s = jnp.dot(kb, qpad_ref[t, kvh, :, g * ATT_TQ:(g + 1) * ATT_TQ], preferred_element_type=F32)
        return jnp.where(valid, s, NEG_BIG)

    def finish(t, kvh, g, s):
        sk = sink_ref[kvh, :, g * ATT_TQ:(g + 1) * ATT_TQ]
        m = jnp.maximum(jnp.max(s, axis=0, keepdims=True), sk)
        p = jnp.exp2(s - m).astype(BF16)
        o = jnp.dot(tiles[t][1][kvh], p, preferred_element_type=F32)
        denom = o[HEAD_DIM:HEAD_DIM + 1, :] + jnp.exp2(sk - m)
        h = N_REP * kvh + g
        ot_ref[t, h * HEAD_DIM:(h + 1) * HEAD_DIM, :] = o[0:HEAD_DIM, :] / denom

    s_next = scores(*work[0])
    for n, item in enumerate(work):
        s_cur = s_next
        if n + 1 < len(work):
            s_next = scores(*work[n + 1])
        finish(*item, s_cur)
    for t in range(WIN_TILES):
        o_ref[t * ATT_TQ:(t + 1) * ATT_TQ, :] = ot_ref[t].T.astype(BF16)


def _window_attn(z, sink_rows, batch, seq):
    rows = WIN_TILES * ATT_TQ
    nq = seq // rows
    return pl.pallas_call(
        functools.partial(_window_attn_kernel, seq=seq),
        grid=(batch, nq),
        in_specs=[
            pl.BlockSpec((rows, GROUP_W), lambda b, i: (b * nq + i, COL_QC // GROUP_W)),
            pl.BlockSpec((seq, KV_W), lambda b, i: (b, COL_KC // KV_W)),
            pl.BlockSpec((seq, KV_W), lambda b, i: (b, COL_VC // KV_W)),
            pl.BlockSpec((N_KV, 1, ATT_M), lambda b, i: (0, 0, 0)),
        ],
        out_specs=pl.BlockSpec((rows, GROUP_W), lambda b, i: (b * nq + i, 0)),
        out_shape=jax.ShapeDtypeStruct((batch * seq, GROUP_W), BF16),
        scratch_shapes=[
            pltpu.VMEM((WIN_TILES, N_KV, KV_W, ATT_M), BF16),
            pltpu.VMEM((WIN_TILES, GROUP_W, ATT_TQ), F32),
        ],
        compiler_params=_cparams(("arbitrary", "arbitrary")),
        name="window_attn",
    )(z, z, z, sink_rows)


def _mix_out_kernel(ab_ref, p_ref, pp_ref, pn_ref, u_ref, up_ref, un_ref,
                    wa_ref, wd_ref, bd_ref, lng_ref, lnb_ref,
                    yb_ref, yc_ref, w_ref, x_ref, g_ref,
                    o_ref, pbuf, ubuf, ush, ya_s, yd_s, *, nseq):
    i = pl.program_id(0)
    keep_prev = jnp.where(i % nseq == 0, 0.0, 1.0)
    keep_next = jnp.where(i % nseq == nseq - 1, 0.0, 1.0)

    tm = MIX_TM
    ubuf[0:HALO, :] = up_ref[...].astype(F32) * keep_prev
    ubuf[HALO:HALO + tm, :] = u_ref[...].astype(F32)
    ubuf[HALO + tm:2 * HALO + tm, :] = un_ref[...].astype(F32) * keep_next
    pbuf[0:HALO, :] = pp_ref[...].astype(F32) * keep_prev
    pbuf[HALO:HALO + tm, :] = p_ref[...].astype(F32)
    pbuf[HALO + tm:2 * HALO + tm, :] = pn_ref[...].astype(F32) * keep_next

    rows = tm + 2 * HALO
    ubuf[rows:rows + F32_SUBLANES, :] = jnp.zeros((F32_SUBLANES, GROUP_W), F32)
    for s in range(1, F32_SUBLANES):
        ush[s - 1, :, :] = ubuf[s:s + rows, :]

    bd, lng, lnb = bd_ref[...], lng_ref[...], lnb_ref[...]

    def mix(r0):
        acc = jnp.zeros((MIX_CH, GROUP_W), F32)
        for k in range(CONF_K):
            lo = r0 + HALO - CONF_K // 2 + k
            s = lo % F32_SUBLANES
            a = lo - s
            tap = ubuf[a:a + MIX_CH, :] if s == 0 else ush[s - 1, a:a + MIX_CH, :]
            wk = wd_ref[k * F32_SUBLANES:(k + 1) * F32_SUBLANES, :]
            acc = acc + (tap.reshape(MIX_CH // F32_SUBLANES, F32_SUBLANES, GROUP_W) * wk[None]
                         ).reshape(MIX_CH, GROUP_W)
        acc = acc + bd
        mu = jnp.mean(acc, axis=-1, keepdims=True)
        xc = acc - mu
        var = jnp.mean(xc * xc, axis=-1, keepdims=True)
        y = (xc * lax.rsqrt(var + LN_EPS)) * lng + lnb
        yd_s[r0:r0 + MIX_CH, :] = (y * jax.nn.sigmoid(y)).astype(BF16)
        pa = jnp.zeros((MIX_CH, GROUP_W), F32)
        for k in range(SCONV_K):
            lo = r0 + HALO - SCONV_K // 2 + k
            pa = pa + wa_ref[k:k + 1, :] * pbuf[lo:lo + MIX_CH, :]
        ya_s[r0:r0 + MIX_CH, :] = (ab_ref[r0:r0 + MIX_CH, :].astype(F32) * pa).astype(BF16)

    def project(rows):
        for r0 in range(rows.start, rows.stop, MIX_CH):
            mix(r0)
        y = jnp.concatenate([ya_s[rows, :], yb_ref[rows, :], yc_ref[rows, :], yd_s[rows, :]], axis=-1)
        return jnp.dot(y, w_ref[...], preferred_element_type=F32)

    def residual(rows, acc):
        ms = jnp.mean(acc * acc, axis=-1, keepdims=True)
        o_ref[rows, :] = x_ref[rows, :] + (acc * lax.rsqrt(ms + RMS_EPS)) * g_ref[...]

    _pipelined(_row_chunks(tm, OUT_RC), project, residual)


def _mix_out(z, yb, yc, seq, wa, wd, bd, lng, lnb, w_bf, x2d, g):
    t = z.shape[0]
    tm = MIX_TM
    nseq = seq // tm
    hb = tm // HALO
    nh = t // HALO

    def cur(col):
        return pl.BlockSpec((tm, GROUP_W), lambda i: (i, col // GROUP_W))

    def prev(col):
        return pl.BlockSpec((HALO, GROUP_W), lambda i: (jnp.maximum(i * hb - 1, 0), col // GROUP_W))

    def nxt(col):
        return pl.BlockSpec((HALO, GROUP_W), lambda i: (jnp.minimum((i + 1) * hb, nh - 1), col // GROUP_W))

    const = lambda shape: pl.BlockSpec(shape, lambda i: (0, 0))
    halo_cols = (COL_P, COL_U)
    in_specs = [cur(COL_AB)]
    for col in halo_cols:
        in_specs += [cur(col), prev(col), nxt(col)]
    in_specs += [const((SCONV_K, GROUP_W)), const((CONF_K * F32_SUBLANES, GROUP_W)), const((1, GROUP_W)),
                 const((1, GROUP_W)), const((1, GROUP_W))]
    yspec = pl.BlockSpec((tm, GROUP_W), lambda i: (i, 0))
    xspec = pl.BlockSpec((tm, D_MODEL), lambda i: (i, 0))
    in_specs += [yspec, yspec,
                 pl.BlockSpec((D_MODEL, D_MODEL), lambda i: (0, 0), pipeline_mode=pl.Buffered(1)),
                 xspec, const((1, D_MODEL))]
    return pl.pallas_call(
        functools.partial(_mix_out_kernel, nseq=nseq),
        grid=(t // tm,),
        in_specs=in_specs,
        out_specs=xspec,
        out_shape=jax.ShapeDtypeStruct((t, D_MODEL), F32),
        scratch_shapes=[pltpu.VMEM((tm + 2 * HALO, GROUP_W), F32),
                        pltpu.VMEM((tm + 2 * HALO + F32_SUBLANES, GROUP_W), F32),
                        pltpu.VMEM((F32_SUBLANES - 1, tm + 2 * HALO, GROUP_W), F32),
                        pltpu.VMEM((tm, GROUP_W), BF16),
                        pltpu.VMEM((tm, GROUP_W), BF16)],
        compiler_params=_cparams(("arbitrary",)),
        name="mix_out",
    )(*([z] * 7), wa, wd, bd, lng, lnb, yb, yc, w_bf, x2d, g)


def _ffn_kernel(x_ref, g1_ref, w1_ref, w2_ref, g2_ref, o_ref, h_ref, acc_ref, *, nf):
    j = pl.program_id(1)
    chunks = _row_chunks(x_ref.shape[0], FFN_RC)

    def norm(rows):
        x = x_ref[rows, :]
        ms = jnp.mean(x * x, axis=-1, keepdims=True)
        h = ((x * lax.rsqrt(ms + RMS_EPS)) * g1_ref[...]).astype(BF16)
        h_ref[rows, :] = h
        return h

    def mlp(h):
        u = jnp.maximum(jnp.dot(h, w1_ref[...], preferred_element_type=F32), 0.0)
        return jnp.dot((u * u).astype(BF16), w2_ref[...], preferred_element_type=F32)

    def first(rows, h):
        acc_ref[rows, :] = mlp(h)

    def last(rows, y):
        ms = jnp.mean(y * y, axis=-1, keepdims=True)
        o_ref[rows, :] = x_ref[rows, :] + (y * lax.rsqrt(ms + RMS_EPS)) * g2_ref[...]

    @pl.when(j == 0)
    def _():
        _pipelined(chunks, norm, first)

    @pl.when(jnp.logical_and(j > 0, j < nf - 1))
    def _():
        acc_ref[...] += mlp(h_ref[...])

    @pl.when(j == nf - 1)
    def _():
        _pipelined(chunks, lambda rows: acc_ref[rows, :] + mlp(h_ref[rows, :]), last)


def _ffn(x2d, g1, w1_bf, w2_bf, g2):
    t = x2d.shape[0]
    tm, tf = FFN_TM, FFN_TF
    nf = D_FF // tf
    xspec = pl.BlockSpec((tm, D_MODEL), lambda i, j: (i, 0))
    gspec = pl.BlockSpec((1, D_MODEL), lambda i, j: (0, 0))
    return pl.pallas_call(
        functools.partial(_ffn_kernel, nf=nf),
        grid=(t // tm, nf),
        in_specs=[xspec, gspec,
                  pl.BlockSpec((D_MODEL, tf), lambda i, j: (0, j)),
                  pl.BlockSpec((tf, D_MODEL), lambda i, j: (j, 0)),
                  gspec],
        out_specs=xspec,
        out_shape=jax.ShapeDtypeStruct((t, D_MODEL), F32),
        scratch_shapes=[pltpu.VMEM((tm, D_MODEL), BF16), pltpu.VMEM((tm, D_MODEL), F32)],
        compiler_params=_cparams(("arbitrary", "arbitrary")),
        name="ffn",
    )(x2d, g1, w1_bf, w2_bf, g2)


def _rope_tables(seq):
    lane = np.arange(LANES)
    t = jnp.arange(seq)
    inv_ax = 1.0 / (ROPE_THETA ** (jnp.arange(0, 32, 2, dtype=F32) / 32))
    pos_row = (t // GRID_W).astype(F32)
    pos_col = (t % GRID_W).astype(F32)
    use_col = ((lane % HEAD_DIM) >= 32)
    pos = jnp.where(use_col[None, :], pos_col[:, None], pos_row[:, None])
    ang = pos * inv_ax[lane % 16][None, :]
    sign_ax = np.where((lane % 32) >= 16, 1.0, -1.0).astype(np.float32)
    cax, sax = jnp.cos(ang), jnp.sin(ang) * sign_ax[None, :]
    inv_li = 1.0 / (ROPE_THETA ** (jnp.arange(0, HEAD_DIM, 2, dtype=F32) / HEAD_DIM))
    ang = t.astype(F32)[:, None] * inv_li[lane % 32][None, :]
    sign_li = np.where((lane % HEAD_DIM) >= 32, 1.0, -1.0).astype(np.float32)
    cli, sli = jnp.cos(ang), jnp.sin(ang) * sign_li[None, :]
    return cax, sax, cli, sli


def _cast_kernel(x_ref, o_ref):
    o_ref[...] = x_ref[...].astype(BF16)


def _to_bf16(w, layer):
    _, r, c = w.shape
    br = min(r, CAST_BLOCK_BYTES // (4 * c))
    assert r % br == 0 and br % BF16_SUBLANES == 0
    return pl.pallas_call(
        _cast_kernel,
        grid=(r // br,),
        in_specs=[pl.BlockSpec((None, br, c), lambda i: (layer, i, 0))],
        out_specs=pl.BlockSpec((br, c), lambda i: (i, 0)),
        out_shape=jax.ShapeDtypeStruct((r, c), BF16),
        compiler_params=_cparams(("arbitrary",)),
        name="cast_bf16",
    )(w)


def _permute_w_in(w):
    o = _ORIG
    order = [("ac", 512), ("av", 512), ("da", 512), ("dg", 512), ("ab", 512), ("qb", 512), ("qc", 512),
             ("kb", 128), ("vb", 128), ("kc", 128), ("vc", 128)]
    return jnp.concatenate([w[:, o[n]:o[n] + wd] for n, wd in order], axis=1).astype(BF16)


def _layer(x2d, batch, seq, tabs, ones_bd, p):
    z = _in_proj(x2d, seq, p["pre_mix_g"], p["w_in"], tabs, p["q_norm_g"], p["k_norm_g"], ones_bd)
    yb = _global_attn(z, batch, seq)
    yc = _window_attn(z, p["sink_rows"], batch, seq)
    x1 = _mix_out(z, yb, yc, seq, p["conv_a_w"], p["conv_d_w"], p["conv_d_b"], p["ln_d_g"], p["ln_d_b"],
                  p["w_out"], x2d, p["post_mix_g"])
    return _ffn(x1, p["pre_ffn_g"], p["w_ff1"], p["w_ff2"], p["post_ffn_g"])


def kernel(x_prompt, x_sample, pre_mix_g, w_in, conv_a_w, q_norm_g, k_norm_g, sink_c, conv_d_w, conv_d_b,
           ln_d_g, ln_d_b, w_out, post_mix_g, pre_ffn_g, w_ff1, w_ff2, post_ffn_g):
    depth = w_in.shape[0]
    lane = np.arange(LANES)
    ones_bd = jnp.asarray((lane[:, None] // HEAD_DIM) == (lane[None, :] // HEAD_DIM), BF16)
    layers = []
    for l in range(depth):
        row = lambda a: a[l].reshape(1, -1).astype(F32)
        sink_rows = jnp.repeat((sink_c[l].astype(F32) * LOG2_E).reshape(N_KV, 1, N_REP), ATT_TQ, axis=2)
        layers.append(dict(
            pre_mix_g=row(pre_mix_g), w_in=_permute_w_in(w_in[l]),
            conv_a_w=conv_a_w[l].astype(F32),
            conv_d_w=jnp.repeat(conv_d_w[l].astype(F32), F32_SUBLANES, axis=0),
            conv_d_b=row(conv_d_b), ln_d_g=row(ln_d_g), ln_d_b=row(ln_d_b),
            q_norm_g=jnp.tile(row(q_norm_g), (1, 2)), k_norm_g=jnp.tile(row(k_norm_g), (1, 2)),
            sink_rows=sink_rows, w_out=_to_bf16(w_out, l), post_mix_g=row(post_mix_g),
            pre_ffn_g=row(pre_ffn_g), w_ff1=_to_bf16(w_ff1, l), w_ff2=_to_bf16(w_ff2, l),
            post_ffn_g=row(post_ffn_g)))
    outs = []
    for x in (x_prompt, x_sample):
        batch, seq, _ = x.shape
        tabs = _rope_tables(seq)
        y = x.reshape(batch * seq, D_MODEL)
        for p in layers:
            y = _layer(y, batch, seq, tabs, ones_bd, p)
        outs.append(y.reshape(batch, seq, D_MODEL))
    return tuple(outs)
```
